```python
import math
import jax, jax.numpy as jnp
from jax import lax
import numpy as np

D_MODEL = 1024
BATCH = 4
SEQ = 8192
DEPTH = 2

GRID_W = 64
CTX_LEN = 256
ROPE_BASE = 10000.0
ROPE_DIM = 64
ROPE_FREQS = ROPE_DIM // 4
NORM_EPS = 1e-6
MASK_VALUE = -1e30

DIFF_HEADS = 4
DIFF_DIM = 64
DIFF_VDIM = 2 * DIFF_DIM
DIFF_QBLOCK = 128

HGRN_HEADS = 4
HGRN_DK = 128
HGRN_DV = 128
HGRN_CHUNK = 64

SWA_Q_HEADS = 8
SWA_KV_HEADS = 2
SWA_GROUP = SWA_Q_HEADS // SWA_KV_HEADS
SWA_DIM = 64
WINDOW = 128
SWA_BLOCK = 128

BRANCH_WIDTH = 512
N_BRANCHES = 3
D_FF = 4 * D_MODEL

SPLIT_SIZES = (DIFF_HEADS * 2 * DIFF_DIM, DIFF_HEADS * 2 * DIFF_DIM, DIFF_HEADS * DIFF_VDIM,
               HGRN_HEADS * HGRN_DK, HGRN_HEADS * HGRN_DK, HGRN_HEADS * HGRN_DK,
               HGRN_HEADS * HGRN_DV, HGRN_HEADS * HGRN_DV,
               SWA_Q_HEADS * SWA_DIM, SWA_KV_HEADS * SWA_DIM, SWA_KV_HEADS * SWA_DIM,
               N_BRANCHES * D_MODEL)
IN_WIDTH = sum(SPLIT_SIZES)

F32 = jnp.float32

kernel_name = "hybrid_diffattn_hgrn2_swa_dit"


def rms_norm(x, g):
    xf = x.astype(F32)
    y = xf * lax.rsqrt(jnp.mean(xf * xf, axis=-1, keepdims=True) + NORM_EPS)
    return (y * g.astype(F32)).astype(x.dtype)


def modulate(h, shift, scale):
    return h * (1.0 + scale) + shift


def axial_rope_tables(n_tokens):
    rows = n_tokens // GRID_W
    row = jnp.repeat(jnp.arange(rows), GRID_W)
    col = jnp.tile(jnp.arange(GRID_W), rows)
    inv_freq = ROPE_BASE ** (-jnp.arange(ROPE_FREQS, dtype=F32) / ROPE_FREQS)
    pos = jnp.stack([row, col], axis=-1).astype(F32)
    ang = pos[:, :, None] * inv_freq
    return jnp.cos(ang), jnp.sin(ang)


def apply_axial_rope(x, cos, sin):
    shp = x.shape
    xf = x.astype(F32).reshape(shp[:-1] + (2, 2, ROPE_FREQS))
    mid = (1,) * (x.ndim - 3)
    cs = cos.reshape((cos.shape[0],) + mid + (2, ROPE_FREQS))
    sn = sin.reshape((sin.shape[0],) + mid + (2, ROPE_FREQS))
    x1, x2 = xf[..., 0, :], xf[..., 1, :]
    out = jnp.stack([x1 * cs - x2 * sn, x2 * cs + x1 * sn], axis=-2)
    return out.reshape(shp).astype(x.dtype)


def split_columns(w):
    idx = np.cumsum(SPLIT_SIZES)[:-1].tolist()
    return jnp.split(w, idx, axis=-1)


def heads(a, *hd):
    return a.reshape(a.shape[:2] + hd)


def diff_weights(q, k, lam):
    s = jnp.einsum('bqhcd,bkhcd->bhcqk', q, k).astype(F32) * (DIFF_DIM ** -0.5)
    p = jax.nn.softmax(s, axis=-1)
    return p[:, :, 0] - lam * p[:, :, 1]


def diff_attention(q_l, k_l, v_l, q_c, k_c, v_c, lam_vecs, subln, lam_init, need_ctx):
    lv = lam_vecs.astype(F32)
    lam = jnp.exp(jnp.sum(lv[0] * lv[1])) - jnp.exp(jnp.sum(lv[2] * lv[3])) + lam_init
    B, S = q_l.shape[:2]
    nb = S // DIFF_QBLOCK
    k_all = jnp.concatenate([k_c, k_l], axis=1)
    v_all = jnp.concatenate([v_c, v_l], axis=1)
    qb = jnp.moveaxis(q_l.reshape(B, nb, DIFF_QBLOCK, DIFF_HEADS, 2, DIFF_DIM), 1, 0)

    def block(qi):
        w = diff_weights(qi, k_all, lam)
        return jnp.einsum('bhqk,bkhe->bqhe', w.astype(v_all.dtype), v_all)

    o_l = jnp.moveaxis(lax.map(block, qb), 0, 1).reshape(B, S, DIFF_HEADS, DIFF_VDIM)
    out_scale = 1.0 - lam_init
    y_l = (rms_norm(o_l, subln) * out_scale).reshape(B, S, -1)
    y_c = None
    if need_ctx:
        w_c = diff_weights(q_c, k_c, lam)
        o_c = jnp.einsum('bhqk,bkhe->bqhe', w_c.astype(v_c.dtype), v_c)
        y_c = (rms_norm(o_c, subln) * out_scale).reshape(o_c.shape[0], o_c.shape[1], -1)
    return y_l, y_c


def hgrn_gates(z, lb):
    z = z.astype(F32)
    k = (1.0 - lb) * jax.nn.sigmoid(-z)
    logf = jnp.log1p(-k)
    return logf, k


def gla_chunk_scan(q, k, v, logf, s0):
    B, L, H, _ = q.shape
    dv = v.shape[-1]
    n = L // HGRN_CHUNK

    def chunks(a):
        return jnp.moveaxis(a.reshape(B, n, HGRN_CHUNK, H, a.shape[-1]), 1, 0)

    causal = jnp.tril(jnp.ones((HGRN_CHUNK, HGRN_CHUNK), dtype=bool))[:, :, None, None]

    def step(S, inp):
        qc, kc, vc, lc = inp
        b = jnp.cumsum(lc, axis=1)
        o_inter = jnp.einsum('bthk,bhkv->bthv', qc * jnp.exp(b), S)
        rel = jnp.where(causal, b[:, :, None] - b[:, None, :], 0.0)
        decay = jnp.where(causal, jnp.exp(rel), 0.0)
        A = jnp.einsum('bthk,bshk,btshk->bhts', qc, kc, decay)
        o_intra = jnp.einsum('bhts,bshv->bthv', A, vc)
        b_last = b[:, -1]
        S_new = S * jnp.exp(b_last)[..., None] + jnp.einsum(
            'bshk,bshv->bhkv', kc * jnp.exp(b_last[:, None] - b), vc)
        return S_new, o_inter + o_intra

    S_fin, o = lax.scan(step, s0, (chunks(q), chunks(k), chunks(v), chunks(logf)))
    return jnp.moveaxis(o, 0, 1).reshape(B, L, H, dv), S_fin


def hgrn_direction(q_c, z_c, v_c, q_l, z_l, v_l, lb):
    logf_c, k_c = hgrn_gates(z_c, lb)
    logf_l, k_l = hgrn_gates(z_l, lb)
    B, _, H, dk = q_c.shape
    s0 = jnp.zeros((B, H, dk, v_c.shape[-1]), F32)
    o_c, s_c = gla_chunk_scan(q_c, k_c, v_c, logf_c, s0)
    o_l, _ = gla_chunk_scan(q_l, k_l, v_l, logf_l, s_c)
    return o_c, o_l


def hgrn2(q_l, zf_l, zb_l, v_l, g_l, q_c, zf_c, zb_c, v_c, g_c, lb, norm_w, need_ctx):
    q_l, q_c = jax.nn.silu(q_l.astype(F32)), jax.nn.silu(q_c.astype(F32))
    v_l, v_c = v_l.astype(F32), v_c.astype(F32)
    rev = lambda a: a[:, ::-1]
    oc_f, ol_f = hgrn_direction(q_c, zf_c, v_c, q_l, zf_l, v_l, lb[0])
    oc_b, ol_b = hgrn_direction(rev(q_c), rev(zb_c), rev(v_c), rev(q_l), rev(zb_l), rev(v_l), lb[1])

    def readout(o, g):
        y = rms_norm(o, norm_w) * jax.nn.silu(g.astype(F32))
        return y.reshape(y.shape[:2] + (-1,)).astype(g.dtype)

    y_l = readout(ol_f + rev(ol_b), g_l)
    y_c = readout(oc_f + rev(oc_b), g_c) if need_ctx else None
    return y_l, y_c


def swa_latent(q, k, v, k_c, v_c, sink):
    B, S = q.shape[:2]
    C = k_c.shape[1]
    nb = S // SWA_BLOCK
    scale = SWA_DIM ** -0.5
    qb = jnp.moveaxis(q.reshape(B, nb, SWA_BLOCK, SWA_KV_HEADS, SWA_GROUP, SWA_DIM), 1, 0)
    pad = ((0, 0), (SWA_BLOCK, SWA_BLOCK), (0, 0), (0, 0))
    kp, vp = jnp.pad(k, pad), jnp.pad(v, pad)
    qi = jnp.arange(SWA_BLOCK)[:, None]
    kj = jnp.arange(3 * SWA_BLOCK)[None, :]
    band = jnp.abs(kj - SWA_BLOCK - qi) <= WINDOW
    sink_l = sink.astype(F32).reshape(SWA_KV_HEADS, SWA_GROUP, 1, 1)

    def block(args):
        n, qn = args
        start = n * SWA_BLOCK
        kw = lax.dynamic_slice_in_dim(kp, start, 3 * SWA_BLOCK, axis=1)
        vw = lax.dynamic_slice_in_dim(vp, start, 3 * SWA_BLOCK, axis=1)
        kpos = start - SWA_BLOCK + kj
        valid = band & (kpos >= 0) & (kpos < S)
        s_win = jnp.einsum('bqhgd,bkhd->bhgqk', qn, kw).astype(F32) * scale
        s_win = jnp.where(valid, s_win, MASK_VALUE)
        s_ctx = jnp.einsum('bqhgd,bkhd->bhgqk', qn, k_c).astype(F32) * scale
        sk = jnp.broadcast_to(sink_l, s_ctx.shape[:-1] + (1,))
        p = jax.nn.softmax(jnp.concatenate([s_ctx, s_win, sk], axis=-1), axis=-1)
        p_ctx = p[..., :C].astype(v.dtype)
        p_win = p[..., C:C + 3 * SWA_BLOCK].astype(v.dtype)
        return (jnp.einsum('bhgqk,bkhd->bqhgd', p_ctx, v_c)
                + jnp.einsum('bhgqk,bkhd->bqhgd', p_win, vw))

    o = lax.map(block, (jnp.arange(nb), qb))
    return jnp.moveaxis(o, 0, 1).reshape(B, S, SWA_Q_HEADS * SWA_DIM)


def swa_context(q, k, v, sink):
    B, C = q.shape[:2]
    qh = q.reshape(B, C, SWA_KV_HEADS, SWA_GROUP, SWA_DIM)
    s = jnp.einsum('bqhgd,bkhd->bhgqk', qh, k).astype(F32) * (SWA_DIM ** -0.5)
    sk = jnp.broadcast_to(sink.astype(F32).reshape(SWA_KV_HEADS, SWA_GROUP, 1, 1), s.shape[:-1] + (1,))
    p = jax.nn.softmax(jnp.concatenate([s, sk], axis=-1), axis=-1)[..., :C]
    o = jnp.einsum('bhgqk,bkhd->bqhgd', p.astype(v.dtype), v)
    return o.reshape(B, C, SWA_Q_HEADS * SWA_DIM)


def merge_branches(ys, gates, w_branch, w_out):
    g = jax.nn.sigmoid(gates.reshape(gates.shape[:2] + (N_BRANCHES, D_MODEL)))
    merged = g[..., 0, :] * (ys[0] @ w_branch[0])
    for j in range(1, N_BRANCHES):
        merged = merged + g[..., j, :] * (ys[j] @ w_branch[j])
    return merged @ w_out


def token_mixer(h_l, h_c, w_in, lam_vecs, subln, lam_init, lb, hgrn_w, sink,
                w_branch, w_out, cos, sin, need_ctx):
    w_parts = split_columns(w_in)
    (dq_l, dk_l, dv_l, hq_l, hff_l, hfb_l, hi_l, hg_l, sq_l, sk_l, sv_l, gt_l) = [h_l @ w for w in w_parts]
    (dq_c, dk_c, dv_c, hq_c, hff_c, hfb_c, hi_c, hg_c, sq_c, sk_c, sv_c, gt_c) = [h_c @ w for w in w_parts]

    y_a_l, y_a_c = diff_attention(
        apply_axial_rope(heads(dq_l, DIFF_HEADS, 2, DIFF_DIM), cos, sin),
        apply_axial_rope(heads(dk_l, DIFF_HEADS, 2, DIFF_DIM), cos, sin),
        heads(dv_l, DIFF_HEADS, DIFF_VDIM),
        heads(dq_c, DIFF_HEADS, 2, DIFF_DIM), heads(dk_c, DIFF_HEADS, 2, DIFF_DIM),
        heads(dv_c, DIFF_HEADS, DIFF_VDIM), lam_vecs, subln, lam_init, need_ctx)

    hk = lambda a: heads(a, HGRN_HEADS, HGRN_DK)
    hv = lambda a: heads(a, HGRN_HEADS, HGRN_DV)
    y_b_l, y_b_c = hgrn2(hk(hq_l), hk(hff_l), hk(hfb_l), hv(hi_l), hv(hg_l),
                         hk(hq_c), hk(hff_c), hk(hfb_c), hv(hi_c), hv(hg_c), lb, hgrn_w, need_ctx)

    k_c = heads(sk_c, SWA_KV_HEADS, SWA_DIM)
    v_c = heads(sv_c, SWA_KV_HEADS, SWA_DIM)
    y_c_l = swa_latent(apply_axial_rope(heads(sq_l, SWA_Q_HEADS, SWA_DIM), cos, sin),
                       apply_axial_rope(heads(sk_l, SWA_KV_HEADS, SWA_DIM), cos, sin),
                       heads(sv_l, SWA_KV_HEADS, SWA_DIM), k_c, v_c, sink)

    out_l = merge_branches((y_a_l, y_b_l, y_c_l), gt_l, w_branch, w_out)
    out_c = None
    if need_ctx:
        y_c_c = swa_context(heads(sq_c, SWA_Q_HEADS, SWA_DIM), k_c, v_c, sink)
        out_c = merge_branches((y_a_c, y_b_c, y_c_c), gt_c, w_branch, w_out)
    return out_l, out_c


def sq_relu_mlp(h, w_up, w_down):
    return jnp.square(jax.nn.relu(h @ w_up)) @ w_down


def setup_inputs(seed: int = 0) -> dict:
    key = jax.random.key(seed)
    ks = jax.random.split(key, 17)

    def nrm(k, shape, scale):
        return jax.random.normal(k, shape, F32) * scale

    return {
        "x": nrm(ks[0], (BATCH, SEQ, D_MODEL), 1.0),
        "c": nrm(ks[1], (BATCH, D_MODEL), 1.0),
        "ctx": nrm(ks[2], (BATCH, CTX_LEN, D_MODEL), 1.0),
        "c_ctx": nrm(ks[3], (D_MODEL,), 1.0),
        "w_ada": nrm(ks[4], (DEPTH, D_MODEL, 6 * D_MODEL), 0.5 * D_MODEL ** -0.5),
        "b_ada": nrm(ks[5], (DEPTH, 6 * D_MODEL), 0.01),
        "norm_g": 1.0 + nrm(ks[6], (DEPTH, 4, D_MODEL), 0.02),
        "w_in": nrm(ks[7], (DEPTH, D_MODEL, IN_WIDTH), D_MODEL ** -0.5),
        "diff_lambda": nrm(ks[8], (DEPTH, 4, DIFF_DIM), 0.1),
        "diff_subln": 1.0 + nrm(ks[9], (DEPTH, DIFF_VDIM), 0.02),
        "hgrn_lb": nrm(ks[10], (DEPTH, 2, HGRN_HEADS * HGRN_DK), 0.1),
        "hgrn_norm": 1.0 + nrm(ks[11], (DEPTH, HGRN_DV), 0.02),
        "swa_sink": nrm(ks[12], (DEPTH, SWA_Q_HEADS), 0.5),
        "w_branch": nrm(ks[13], (DEPTH, N_BRANCHES, BRANCH_WIDTH, D_MODEL), BRANCH_WIDTH ** -0.5),
        "w_out": nrm(ks[14], (DEPTH, D_MODEL, D_MODEL), D_MODEL ** -0.5),
        "w_mlp_up": nrm(ks[15], (DEPTH, D_MODEL, D_FF), D_MODEL ** -0.5),
        "w_mlp_down": nrm(ks[16], (DEPTH, D_FF, D_MODEL), D_FF ** -0.5),
    }


def reference(x, c, ctx, c_ctx, w_ada, b_ada, norm_g, w_in, diff_lambda, diff_subln,
              hgrn_lb, hgrn_norm, swa_sink, w_branch, w_out, w_mlp_up, w_mlp_down):
    cos, sin = axial_rope_tables(x.shape[1])
    lb_soft = jax.nn.softmax(hgrn_lb.astype(F32), axis=0)
    lower_bounds = (jnp.cumsum(lb_soft, axis=0) - lb_soft[0:1]).reshape(
        lb_soft.shape[:2] + (HGRN_HEADS, HGRN_DK))
    c_act = jax.nn.silu(c)
    cc_act = jax.nn.silu(c_ctx)
    for l in range(DEPTH):
        need_ctx = l < DEPTH - 1
        lam_init = 0.8 - 0.6 * math.exp(-0.3 * l)
        m_l = jnp.split((c_act @ w_ada[l] + b_ada[l])[:, None, :], 6, axis=-1)
        m_c = jnp.split((cc_act @ w_ada[l] + b_ada[l])[None, None, :], 6, axis=-1)

        h_l = modulate(rms_norm(x, norm_g[l, 0]), m_l[0], m_l[1])
        h_c = modulate(rms_norm(ctx, norm_g[l, 0]), m_c[0], m_c[1])
        y_l, y_c = token_mixer(h_l, h_c, w_in[l], diff_lambda[l], diff_subln[l], lam_init,
                               lower_bounds[l], hgrn_norm[l], swa_sink[l], w_branch[l], w_out[l],
                               cos, sin, need_ctx)
        x = x + m_l[2] * rms_norm(y_l, norm_g[l, 1])
        h_l = modulate(rms_norm(x, norm_g[l, 2]), m_l[3], m_l[4])
        x = x + m_l[5] * rms_norm(sq_relu_mlp(h_l, w_mlp_up[l], w_mlp_down[l]), norm_g[l, 3])

        if need_ctx:
            ctx = ctx + m_c[2] * rms_norm(y_c, norm_g[l, 1])
            h_c = modulate(rms_norm(ctx, norm_g[l, 2]), m_c[3], m_c[4])
            ctx = ctx + m_c[5] * rms_norm(sq_relu_mlp(h_c, w_mlp_up[l], w_mlp_down[l]), norm_g[l, 3])
    return x
```

```python
import functools
import math

import numpy as np
import jax
import jax.numpy as jnp
from jax import lax
from jax.experimental import pallas as pl
from jax.experimental.pallas import tpu as pltpu

F32 = jnp.float32
BF16 = jnp.bfloat16

D_MODEL = 1024
GRID_W = 64
CTX_LEN = 256
ROPE_BASE = 10000.0
ROPE_FREQS = 16
NORM_EPS = 1e-6
MASK_VALUE = -1e30

DIFF_HEADS = 4
DIFF_DIM = 64
HGRN_HEADS = 4
HGRN_DK = 128
SWA_KV_HEADS = 2
SWA_GROUP = 4
SWA_DIM = 64
WINDOW = 128
D_FF = 4 * D_MODEL

LANES = 128
TOKEN_TILE = 256
HGRN_CHUNK = 64
HGRN_SUB = 8
ATT_TQ = 512
ATT_TK = 512
SWA_TQ = 256
VMEM_LIMIT = 56 * 1024 * 1024

_OFF = {}
_o = 0
for _n, _w in (("dq", 512), ("dk", 512), ("dv", 512), ("hq", 512), ("hff", 512), ("hfb", 512),
               ("hi", 512), ("hg", 512), ("sq", 512), ("sk", 128), ("sv", 128), ("gt", 3072)):
    _OFF[_n] = (_o, _o + _w)
    _o += _w


def _sigmoid(x):
    return 1.0 / (1.0 + jnp.exp(-x))


def _dot(a, b):
    return jnp.dot(a, b, preferred_element_type=F32)


def _dot_nt(a, b):
    return lax.dot_general(a, b, (((1,), (1,)), ((), ())), preferred_element_type=F32)


def _dot_tn(a, b):
    return lax.dot_general(a, b, (((0,), (0,)), ((), ())), preferred_element_type=F32)


def _rms(x, eps=NORM_EPS):
    return x * lax.rsqrt(jnp.mean(x * x, axis=-1, keepdims=True) + eps)


def _cparams(sem):
    return pltpu.CompilerParams(dimension_semantics=sem, vmem_limit_bytes=VMEM_LIMIT)


def _ada_kernel(c_ref, w_ref, b_ref, o_ref):
    c = c_ref[...]
    a = c * _sigmoid(c)
    o_ref[0] = _dot(a, w_ref[0]) + b_ref[0]


def _ada(cvec, w_ada, b_ada):
    depth, d, n = w_ada.shape
    tn = 1024
    return pl.pallas_call(
        _ada_kernel,
        grid=(depth, n // tn),
        in_specs=[pl.BlockSpec((8, d), lambda l, j: (0, 0)),
                  pl.BlockSpec((1, d, tn), lambda l, j: (l, 0, j)),
                  pl.BlockSpec((1, 1, tn), lambda l, j: (l, 0, j))],
        out_specs=pl.BlockSpec((1, 8, tn), lambda l, j: (l, 0, j)),
        out_shape=jax.ShapeDtypeStruct((depth, 8, n), F32),
        compiler_params=_cparams(("arbitrary", "arbitrary")),
        name="ada",
    )(cvec, w_ada, b_ada.reshape(depth, 1, n))


def _lb_kernel(p_ref, o_ref):
    p = p_ref[...]
    e = jnp.exp(p - jnp.max(p, axis=0, keepdims=True))
    sm = e / jnp.sum(e, axis=0, keepdims=True)
    run = jnp.zeros_like(sm[0])
    for l in range(p.shape[0]):
        run = run + sm[l]
        o_ref[l] = run - sm[0]


def _lower_bounds(hgrn_lb):
    return pl.pallas_call(
        _lb_kernel,
        out_shape=jax.ShapeDtypeStruct(hgrn_lb.shape, F32),
        name="hgrn_lower_bounds",
    )(hgrn_lb.astype(F32))


def _mod_slices(mod_ref, idx):
    return mod_ref[0, 0, :, idx * D_MODEL:(idx + 1) * D_MODEL]


def _norm_mod_kernel(x_ref, g_ref, mod_ref, o_ref):
    y = _rms(x_ref[0]) * g_ref[...]
    o_ref[0] = (y * (1.0 + _mod_slices(mod_ref, 1)) + _mod_slices(mod_ref, 0)).astype(o_ref.dtype)


def _mod_spec(n_lat_tiles):
    return pl.BlockSpec((1, 1, 1, 6 * D_MODEL), lambda b, i: (b, i // n_lat_tiles, 0, 0))


def _norm_mod(xc, g, modsel, n_lat_tiles):
    B, T, D = xc.shape
    return pl.pallas_call(
        _norm_mod_kernel,
        grid=(B, T // TOKEN_TILE),
        in_specs=[pl.BlockSpec((1, TOKEN_TILE, D), lambda b, i: (b, i, 0)),
                  pl.BlockSpec((1, D), lambda b, i: (0, 0)),
                  _mod_spec(n_lat_tiles)],
        out_specs=pl.BlockSpec((1, TOKEN_TILE, D), lambda b, i: (b, i, 0)),
        out_shape=jax.ShapeDtypeStruct((B, T, D), BF16),
        compiler_params=_cparams(("parallel", "parallel")),
        name="norm_mod",
    )(xc, g.reshape(1, D), modsel)


def _rope_partner(x):
    lane = lax.broadcasted_iota(jnp.int32, x.shape, 1)
    first = (lane % 32) < ROPE_FREQS
    return jnp.where(first, pltpu.roll(x, LANES - ROPE_FREQS, 1), pltpu.roll(x, ROPE_FREQS, 1))


def _proj_kernel(h_ref, w_ref, *rest, kind):
    acc = _dot(h_ref[0], w_ref[...])
    if kind == "plain":
        (o_ref,) = rest
        o_ref[0] = acc.astype(o_ref.dtype)
    elif kind == "silu":
        (o_ref,) = rest
        o_ref[0] = (acc * _sigmoid(acc)).astype(o_ref.dtype)
    elif kind == "sigmoid":
        (o_ref,) = rest
        o_ref[0] = _sigmoid(acc).astype(o_ref.dtype)
    elif kind == "rope":
        cos_ref, sin_ref, o_ref = rest
        cos, sin = cos_ref[...], sin_ref[...]
        for j in range(acc.shape[1] // LANES):
            a = acc[:, j * LANES:(j + 1) * LANES]
            o_ref[0, :, j * LANES:(j + 1) * LANES] = (a * cos + _rope_partner(a) * sin).astype(o_ref.dtype)
    elif kind == "hgate":
        lb_ref, logf_ref, kk_ref = rest
        kk = (1.0 - lb_ref[...]) * _sigmoid(-acc)
        kk_ref[0] = kk
        logf_ref[0] = jnp.log1p(-kk)
    else:
        raise ValueError(kind)


def _proj(h, w, kind, tn, extra=(), out_dtype=BF16):
    B, T, D = h.shape
    n = w.shape[1]
    assert n % tn == 0 and T % TOKEN_TILE == 0
    grid = (n // tn, B, T // TOKEN_TILE)
    in_specs = [pl.BlockSpec((1, TOKEN_TILE, D), lambda j, b, i: (b, i, 0)),
                pl.BlockSpec((D, tn), lambda j, b, i: (0, j))]
    out_spec = pl.BlockSpec((1, TOKEN_TILE, tn), lambda j, b, i: (b, i, j))
    if kind == "rope":
        in_specs += [pl.BlockSpec((TOKEN_TILE, LANES), lambda j, b, i: (i, 0))] * 2
        out_specs, out_shape = out_spec, jax.ShapeDtypeStruct((B, T, n), out_dtype)
    elif kind == "hgate":
        in_specs += [pl.BlockSpec((1, tn), lambda j, b, i: (0, j))]
        out_specs = [out_spec, out_spec]
        out_shape = [jax.ShapeDtypeStruct((B, T, n), F32)] * 2
    else:
        out_specs, out_shape = out_spec, jax.ShapeDtypeStruct((B, T, n), out_dtype)
    return pl.pallas_call(
        functools.partial(_proj_kernel, kind=kind),
        grid=grid, in_specs=in_specs, out_specs=out_specs, out_shape=out_shape,
        compiler_params=_cparams(("parallel", "parallel", "parallel")),
        name="proj_" + kind,
    )(h, w, *extra)


def _diff_attn_kernel(lam_ref, subln_ref, q_ref, kc_ref, vc_ref, *rest, lam_init, with_latent):
    if with_latent:
        k_ref, v_ref, o_ref, m_sc, l_sc, acc_sc = rest
        ki, nk = pl.program_id(3), pl.num_programs(3)
    else:
        o_ref, m_sc, l_sc, acc_sc = rest
        ki, nk = 0, 1
    q = q_ref[0] * jnp.asarray(DIFF_DIM ** -0.5, q_ref.dtype)
    lane = lax.broadcasted_iota(jnp.int32, (1, LANES), 1)

    def update(k, v):
        for c in range(2):
            kc = jnp.where((lane >= DIFF_DIM) == bool(c), k, jnp.zeros_like(k))
            s = _dot_nt(q, kc)
            m_prev = m_sc[c]
            m_new = jnp.maximum(m_prev, jnp.max(s, axis=-1, keepdims=True))
            alpha = jnp.exp(m_prev - m_new)
            p = jnp.exp(s - m_new[:, :1])
            l_sc[c] = alpha * l_sc[c] + jnp.sum(p, axis=-1, keepdims=True)
            acc_sc[c] = alpha * acc_sc[c] + _dot(p.astype(v.dtype), v)
            m_sc[c] = m_new

    def init_and_ctx():
        m_sc[...] = jnp.full(m_sc.shape, MASK_VALUE, F32)
        l_sc[...] = jnp.zeros(l_sc.shape, F32)
        acc_sc[...] = jnp.zeros(acc_sc.shape, F32)
        update(kc_ref[0], vc_ref[0])

    def finalize():
        lv = lam_ref[...]
        s01 = jnp.sum(lv[0:1] * lv[1:2], axis=-1, keepdims=True)
        s23 = jnp.sum(lv[2:3] * lv[3:4], axis=-1, keepdims=True)
        lam = jnp.exp(s01) - jnp.exp(s23) + lam_init
        o = acc_sc[0] / l_sc[0] - lam * (acc_sc[1] / l_sc[1])
        o_ref[0] = (_rms(o) * subln_ref[...] * (1.0 - lam_init)).astype(o_ref.dtype)

    if with_latent:
        pl.when(ki == 0)(init_and_ctx)
        update(k_ref[0], v_ref[0])
        pl.when(ki == nk - 1)(finalize)
    else:
        init_and_ctx()
        finalize()


def _diff_attention(g1, g2, lam_vecs, subln, lam_init, n_lat, latent):
    B, T, _ = g1.shape
    ctx_blk = n_lat // CTX_LEN
    nh = DIFF_HEADS
    scratch = lambda tq: [pltpu.VMEM((2, tq, LANES), F32)] * 3
    if latent:
        tq, tk = ATT_TQ, ATT_TK
        grid = (B, nh, n_lat // tq, n_lat // tk)
        in_specs = [pl.BlockSpec((4, DIFF_DIM), lambda b, h, i, j: (0, 0)),
                    pl.BlockSpec((1, LANES), lambda b, h, i, j: (0, 0)),
                    pl.BlockSpec((1, tq, LANES), lambda b, h, i, j: (b, i, h)),
                    pl.BlockSpec((1, CTX_LEN, LANES), lambda b, h, i, j: (b, ctx_blk, nh + h)),
                    pl.BlockSpec((1, CTX_LEN, LANES), lambda b, h, i, j: (b, ctx_blk, nh + h)),
                    pl.BlockSpec((1, tk, LANES), lambda b, h, i, j: (b, j, nh + h)),
                    pl.BlockSpec((1, tk, LANES), lambda b, h, i, j: (b, j, nh + h))]
        out_specs = pl.BlockSpec((1, tq, LANES), lambda b, h, i, j: (b, i, h))
        out_shape = jax.ShapeDtypeStruct((B, n_lat, nh * LANES), BF16)
        args = (lam_vecs, subln.reshape(1, LANES), g1, g1, g2, g1, g2)
        sem = ("parallel", "parallel", "parallel", "arbitrary")
    else:
        tq = CTX_LEN
        grid = (B, nh)
        in_specs = [pl.BlockSpec((4, DIFF_DIM), lambda b, h: (0, 0)),
                    pl.BlockSpec((1, LANES), lambda b, h: (0, 0)),
                    pl.BlockSpec((1, tq, LANES), lambda b, h: (b, ctx_blk, h)),
                    pl.BlockSpec((1, CTX_LEN, LANES), lambda b, h: (b, ctx_blk, nh + h)),
                    pl.BlockSpec((1, CTX_LEN, LANES), lambda b, h: (b, ctx_blk, nh + h))]
        out_specs = pl.BlockSpec((1, tq, LANES), lambda b, h: (b, 0, h))
        out_shape = jax.ShapeDtypeStruct((B, CTX_LEN, nh * LANES), BF16)
        args = (lam_vecs, subln.reshape(1, LANES), g1, g1, g2)
        sem = ("parallel", "parallel")
    return pl.pallas_call(
        functools.partial(_diff_attn_kernel, lam_init=lam_init, with_latent=latent),
        grid=grid, in_specs=in_specs, out_specs=out_specs, out_shape=out_shape,
        scratch_shapes=scratch(tq),
        compiler_params=_cparams(sem),
        name="diff_attn_latent" if latent else "diff_attn_ctx",
    )(*args)


def _swa_kernel(sink_ref, q_ref, kc_ref, vc_ref, *rest, n_lat, latent):
    g = pl.program_id(1)
    if latent:
        kp_ref, kk_ref, kn_ref, vp_ref, vv_ref, vn_ref, o_ref = rest
        i = pl.program_id(2)
        k = jnp.concatenate([kc_ref[0], kp_ref[0], kk_ref[0], kn_ref[0]], axis=0)
        v = jnp.concatenate([vc_ref[0], vp_ref[0], vv_ref[0], vn_ref[0]], axis=0)
        tq = q_ref.shape[1]
        nkeys = k.shape[0]
        r = lax.broadcasted_iota(jnp.int32, (tq, nkeys), 0)
        j = lax.broadcasted_iota(jnp.int32, (tq, nkeys), 1) - CTX_LEN
        rel = j - WINDOW - r
        kpos = i * tq - WINDOW + j
        valid = (j < 0) | ((rel <= WINDOW) & (rel >= -WINDOW) & (kpos >= 0) & (kpos < n_lat))
    else:
        (o_ref,) = rest
        k, v = kc_ref[0], vc_ref[0]
        valid = None
    lane = lax.broadcasted_iota(jnp.int32, (1, LANES), 1)
    scale = jnp.asarray(SWA_DIM ** -0.5, q_ref.dtype)
    for pair in range(SWA_GROUP // 2):
        qp = q_ref[0, :, pair * LANES:(pair + 1) * LANES] * scale
        out = None
        for e in range(2):
            half = (lane >= SWA_DIM) == bool(e)
            s = _dot_nt(qp, jnp.where(half, k, jnp.zeros_like(k)))
            if valid is not None:
                s = jnp.where(valid, s, MASK_VALUE)
            sink = sink_ref[g * SWA_GROUP + pair * 2 + e]
            m = jnp.maximum(jnp.max(s, axis=-1, keepdims=True), sink)
            p = jnp.exp(s - m)
            denom = jnp.sum(p, axis=-1, keepdims=True) + jnp.exp(sink - m)
            pv = _dot(p.astype(v.dtype), jnp.where(half, v, jnp.zeros_like(v))) / denom
            out = pv if out is None else out + pv
        o_ref[0, :, pair * LANES:(pair + 1) * LANES] = out.astype(o_ref.dtype)


def _swa(g1, g2, sink, n_lat, latent):
    B, T, _ = g1.shape
    ctx_blk = n_lat // CTX_LEN
    qcol = 1024 // (2 * LANES)
    kcol = 1536 // LANES
    vcol = 1024 // LANES
    smem = pl.BlockSpec(memory_space=pltpu.SMEM)
    if latent:
        tq = SWA_TQ
        half = tq // 2
        nq = n_lat // tq
        last_half = n_lat // half - 1
        grid = (B, SWA_KV_HEADS, nq)
        ctx_k = pl.BlockSpec((1, CTX_LEN, LANES), lambda b, g, i: (b, ctx_blk, kcol + g))
        ctx_v = pl.BlockSpec((1, CTX_LEN, LANES), lambda b, g, i: (b, ctx_blk, vcol + g))

        def win(col):
            return [pl.BlockSpec((1, half, LANES), lambda b, g, i: (b, jnp.maximum(2 * i - 1, 0), col + g)),
                    pl.BlockSpec((1, tq, LANES), lambda b, g, i: (b, i, col + g)),
                    pl.BlockSpec((1, half, LANES), lambda b, g, i: (b, jnp.minimum(2 * i + 2, last_half), col + g))]

        in_specs = [smem, pl.BlockSpec((1, tq, 2 * LANES), lambda b, g, i: (b, i, qcol + g)),
                    ctx_k, ctx_v] + win(kcol) + win(vcol)
        out_specs = pl.BlockSpec((1, tq, 2 * LANES), lambda b, g, i: (b, i, g))
        out_shape = jax.ShapeDtypeStruct((B, n_lat, 4 * LANES), BF16)
        args = (sink, g1, g1, g2, g1, g1, g1, g2, g2, g2)
        sem = ("parallel", "parallel", "parallel")
    else:
        grid = (B, SWA_KV_HEADS)
        in_specs = [smem, pl.BlockSpec((1, CTX_LEN, 2 * LANES), lambda b, g: (b, ctx_blk, qcol + g)),
                    pl.BlockSpec((1, CTX_LEN, LANES), lambda b, g: (b, ctx_blk, kcol + g)),
                    pl.BlockSpec((1, CTX_LEN, LANES), lambda b, g: (b, ctx_blk, vcol + g))]
        out_specs = pl.BlockSpec((1, CTX_LEN, 2 * LANES), lambda b, g: (b, 0, g))
        out_shape = jax.ShapeDtypeStruct((B, CTX_LEN, 4 * LANES), BF16)
        args = (sink, g1, g1, g2)
        sem = ("parallel", "parallel")
    return pl.pallas_call(
        functools.partial(_swa_kernel, n_lat=n_lat, latent=latent),
        grid=grid, in_specs=in_specs, out_specs=out_specs, out_shape=out_shape,
        compiler_params=_cparams(sem),
        name="swa_latent" if latent else "swa_ctx",
    )(*args)


def _hgrn_constants(reverse):
    C, sub = HGRN_CHUNK, HGRN_SUB
    t = np.arange(C)[:, None]
    u = np.arange(C)[None, :]
    blocks = [(u <= t)]
    masks = [((u // sub) == (t // sub)) & (u <= t)]
    h = sub
    while h < C:
        start = (t // (2 * h)) * (2 * h)
        anchor = start + h - 1
        is_q = ((t // h) % 2) == 1
        blocks.append(np.where(is_q, (u > anchor) & (u <= t), (u > t) & (u <= anchor)))
        masks.append(is_q & ((u // (2 * h)) == (t // (2 * h))) & (((u // h) % 2) == 0))
        h *= 2
    blocks.append(u > t)
    if reverse:
        blocks = [m[::-1, ::-1] for m in blocks]
        masks = [m[::-1, ::-1] for m in masks]
    cs = np.concatenate(blocks, axis=0).astype(np.float32)
    mk = np.concatenate(masks, axis=0).astype(np.float32)
    rows = np.arange(sub * HGRN_DK)[:, None] // HGRN_DK
    esel = (rows == (np.arange(C)[None, :] % sub)).astype(np.float32)
    return cs, mk, esel


def _hgrn_kernel(cs_ref, mk_ref, esel_ref, q_ref, lf_ref, kk_ref, v_ref, o_ref, st_sc, *, reverse):
    C, sub = HGRN_CHUNK, HGRN_SUB
    nlev = cs_ref.shape[0] // C - 2

    @pl.when(pl.program_id(1) == 0)
    def _():
        st_sc[...] = jnp.zeros(st_sc.shape, F32)

    cs = cs_ref[...]
    esel = esel_ref[...]
    masks = [mk_ref[n * C:(n + 1) * C, :] > 0.5 for n in range(nlev + 1)]
    for hh in range(HGRN_HEADS):
        sl = slice(hh * HGRN_DK, (hh + 1) * HGRN_DK)
        q = q_ref[0, :, sl].astype(F32)
        kk = kk_ref[0, :, sl]
        v = v_ref[0, :, sl]
        lf = lf_ref[0, :, sl]
        lf_hi = lf.astype(BF16)
        lf_lo = (lf - lf_hi.astype(F32)).astype(BF16)
        sums = _dot(cs, lf_hi) + _dot(cs, lf_lo)
        b = sums[0:C]
        q3, k3, b3 = (a.reshape(C // sub, sub, HGRN_DK) for a in (q, kk, b))
        tiles = []
        for sg in range(sub):
            d = jnp.minimum(b3 - b3[:, sg:sg + 1, :], 0.0)
            tiles.append((q3 * k3[:, sg:sg + 1, :] * jnp.exp(d)).reshape(C, HGRN_DK).astype(BF16))
        a_mat = jnp.where(masks[0], _dot(jnp.concatenate(tiles, axis=-1), esel), 0.0)
        for n in range(nlev):
            w = jnp.exp(sums[(n + 1) * C:(n + 2) * C])
            a_mat = a_mat + jnp.where(masks[n + 1],
                                      _dot_nt((q * w).astype(BF16), (kk * w).astype(BF16)), 0.0)
        st = st_sc[hh]
        o = _dot_nt((q * jnp.exp(b)).astype(BF16), st.astype(BF16)) + _dot(a_mat.astype(BF16), v)
        o_ref[0, :, sl] = o
        rem = sums[(nlev + 1) * C:(nlev + 2) * C]
        total = b[0:1] if reverse else b[C - 1:C]
        st_sc[hh] = st * jnp.exp(total) + _dot_tn(v, (kk * jnp.exp(rem)).astype(BF16))


def _hgrn_direction(g3, logf, kk, g2, n_lat, reverse):
    B, T, _ = g3.shape
    C = HGRN_CHUNK
    width = HGRN_HEADS * HGRN_DK
    nl, nc = n_lat // C, CTX_LEN // C
    cs, mk, esel = _hgrn_constants(reverse)
    d = 1 if reverse else 0

    if reverse:
        row = lambda i: nl + nc - 1 - i
    else:
        row = lambda i: jnp.where(i < nc, nl + i, i - nc)

    def tok(col):
        return pl.BlockSpec((1, C, width), lambda b, i: (b, row(i), col))

    const = lambda a: pl.BlockSpec(a.shape, lambda b, i: (0, 0))
    return pl.pallas_call(
        functools.partial(_hgrn_kernel, reverse=reverse),
        grid=(B, T // C),
        in_specs=[const(cs), const(mk), const(esel), tok(0), tok(d), tok(d), tok(0)],
        out_specs=tok(0),
        out_shape=jax.ShapeDtypeStruct((B, T, width), F32),
        scratch_shapes=[pltpu.VMEM((HGRN_HEADS, HGRN_DK, HGRN_DK), F32)],
        compiler_params=_cparams(("parallel", "arbitrary")),
        name="hgrn_bwd" if reverse else "hgrn_fwd",
    )(jnp.asarray(cs, BF16), jnp.asarray(mk, F32), jnp.asarray(esel, BF16), g3, logf, kk, g2)


def _merge_kernel(x_ref, ya_ref, of_ref, ob_ref, sg_ref, yc_ref, gt_ref, mod_ref, ng_ref, hn_ref,
                  wb_ref, wo_ref, xo_ref, h_ref):
    o = of_ref[0] + ob_ref[0]
    sg = sg_ref[0].astype(F32)
    yb = jnp.concatenate(
        [_rms(o[:, h * LANES:(h + 1) * LANES]) * hn_ref[...] * sg[:, h * LANES:(h + 1) * LANES]
         for h in range(HGRN_HEADS)], axis=-1).astype(BF16)
    ys = (ya_ref[0], yb, yc_ref[0])
    merged = None
    for j in range(3):
        gate = gt_ref[0, :, j * D_MODEL:(j + 1) * D_MODEL].astype(F32)
        term = gate * _dot(ys[j], wb_ref[j])
        merged = term if merged is None else merged + term
    y = _dot(merged.astype(BF16), wo_ref[...])
    x_new = x_ref[0] + _mod_slices(mod_ref, 2) * (_rms(y) * ng_ref[1:2])
    xo_ref[0] = x_new
    h2 = _rms(x_new) * ng_ref[2:3]
    h_ref[0] = (h2 * (1.0 + _mod_slices(mod_ref, 4)) + _mod_slices(mod_ref, 3)).astype(h_ref.dtype)


def _merge(xc, ya, o_f, o_b, g3, yc, g5, modsel, norm_g, hgrn_norm, wb, wo, n_tiles, n_lat_tiles):
    B, _, D = xc.shape
    tm = TOKEN_TILE
    rows = n_tiles * tm
    tokw = lambda w, col=0: pl.BlockSpec((1, tm, w), lambda b, i: (b, i, col))
    full = lambda a: pl.BlockSpec(a.shape, lambda b, i: (0,) * a.ndim)
    hn = hgrn_norm.reshape(1, LANES)
    return pl.pallas_call(
        _merge_kernel,
        grid=(B, n_tiles),
        in_specs=[tokw(D), tokw(512), tokw(512), tokw(512), tokw(512, 1), tokw(512), tokw(3 * D),
                  _mod_spec(n_lat_tiles), full(norm_g), full(hn), full(wb), full(wo)],
        out_specs=[tokw(D), tokw(D)],
        out_shape=[jax.ShapeDtypeStruct((B, rows, D), F32), jax.ShapeDtypeStruct((B, rows, D), BF16)],
        compiler_params=_cparams(("parallel", "parallel")),
        name="merge",
    )(xc, ya, o_f, o_b, g3, yc, g5, modsel, norm_g, hn, wb, wo)


def _mlp_kernel(x_ref, h_ref, mod_ref, ng_ref, wu_ref, wd_ref, o_ref):
    u = jnp.maximum(_dot(h_ref[0], wu_ref[...]), 0.0)
    y = _dot((u * u).astype(BF16), wd_ref[...])
    o_ref[0] = x_ref[0] + _mod_slices(mod_ref, 5) * (_rms(y) * ng_ref[3:4])


def _mlp(x_new, h2, modsel, norm_g, wu, wd, n_lat_tiles):
    B, rows, D = x_new.shape
    tm = TOKEN_TILE
    tok = pl.BlockSpec((1, tm, D), lambda b, i: (b, i, 0))
    full = lambda a: pl.BlockSpec(a.shape, lambda b, i: (0,) * a.ndim)
    return pl.pallas_call(
        _mlp_kernel,
        grid=(B, rows // tm),
        in_specs=[tok, tok, _mod_spec(n_lat_tiles), full(norm_g), full(wu), full(wd)],
        out_specs=tok,
        out_shape=jax.ShapeDtypeStruct((B, rows, D), F32),
        compiler_params=_cparams(("parallel", "parallel")),
        name="mlp",
    )(x_new, h2, modsel, norm_g, wu, wd)


def _rope_tables(n_lat):
    t = jnp.arange(n_lat)
    pos = jnp.stack([t // GRID_W, t % GRID_W], axis=-1).astype(F32)
    inv_freq = ROPE_BASE ** (-jnp.arange(ROPE_FREQS, dtype=F32) / ROPE_FREQS)
    ang = pos[:, :, None] * inv_freq
    cos, sin = jnp.cos(ang), jnp.sin(ang)
    cos64 = jnp.concatenate([cos, cos], axis=-1).reshape(n_lat, 4 * ROPE_FREQS)
    sin64 = jnp.concatenate([-sin, sin], axis=-1).reshape(n_lat, 4 * ROPE_FREQS)
    cos_t = jnp.concatenate([jnp.tile(cos64, (1, 2)), jnp.ones((CTX_LEN, LANES), F32)], axis=0)
    sin_t = jnp.concatenate([jnp.tile(sin64, (1, 2)), jnp.zeros((CTX_LEN, LANES), F32)], axis=0)
    return cos_t, sin_t


def _cols(w, *names):
    return [w[:, _OFF[n][0]:_OFF[n][1]] for n in names]


def _dup_heads(w):
    a, b = w[:, :SWA_DIM], w[:, SWA_DIM:]
    return jnp.concatenate([a, a, b, b], axis=1)


def kernel(x, c, ctx, c_ctx, w_ada, b_ada, norm_g, w_in, diff_lambda, diff_subln, hgrn_lb, hgrn_norm,
           swa_sink, w_branch, w_out, w_mlp_up, w_mlp_down):
    B, S, D = x.shape
    depth = w_ada.shape[0]
    assert D == D_MODEL and ctx.shape[1] == CTX_LEN and S % max(ATT_TQ, ATT_TK) == 0
    n_lat_tiles = S // TOKEN_TILE
    n_tiles = n_lat_tiles + 1

    cos_t, sin_t = _rope_tables(S)
    cvec = jnp.concatenate([c, c_ctx[None], jnp.zeros((8 - B - 1, D), F32)], axis=0)
    mods = _ada(cvec, w_ada, b_ada)
    lower = _lower_bounds(hgrn_lb)
    xc = jnp.concatenate([x, ctx], axis=1)

    for l in range(depth):
        need_ctx = l < depth - 1
        lam_init = 0.8 - 0.6 * math.exp(-0.3 * l)
        modsel = jnp.stack([mods[l, :B], jnp.broadcast_to(mods[l, B], (B, 6 * D))], axis=1)[:, :, None, :]
        wl = w_in[l]
        dq, dk, dv, hq, hff, hfb, hi, hg, sq, sk, sv, gt = _cols(
            wl, "dq", "dk", "dv", "hq", "hff", "hfb", "hi", "hg", "sq", "sk", "sv", "gt")
        w1 = jnp.concatenate([dq, dk, sq, _dup_heads(sk)], axis=1).astype(BF16)
        w2 = jnp.concatenate([hi, dv, _dup_heads(sv)], axis=1).astype(BF16)
        w3 = jnp.concatenate([hq, hg], axis=1).astype(BF16)
        w4 = jnp.concatenate([hff, hfb], axis=1).astype(BF16)
        w5 = gt.astype(BF16)

        h = _norm_mod(xc, norm_g[l, 0], modsel, n_lat_tiles)
        g1 = _proj(h, w1, "rope", 896, extra=(cos_t, sin_t))
        g2 = _proj(h, w2, "plain", 1280)
        g3 = _proj(h, w3, "silu", 1024)
        logf, kk = _proj(h, w4, "hgate", 1024, extra=(lower[l].reshape(1, 1024),))
        g5 = _proj(h, w5, "sigmoid", 1024)

        ya = _diff_attention(g1, g2, diff_lambda[l], diff_subln[l], lam_init, S, latent=True)
        yc = _swa(g1, g2, swa_sink[l], S, latent=True)
        if need_ctx:
            ya = jnp.concatenate(
                [ya, _diff_attention(g1, g2, diff_lambda[l], diff_subln[l], lam_init, S, latent=False)], axis=1)
            yc = jnp.concatenate([yc, _swa(g1, g2, swa_sink[l], S, latent=False)], axis=1)
        o_f = _hgrn_direction(g3, logf, kk, g2, S, reverse=False)
        o_b = _hgrn_direction(g3, logf, kk, g2, S, reverse=True)

        nt = n_tiles if need_ctx else n_lat_tiles
        x_new, h2 = _merge(xc, ya, o_f, o_b, g3, yc, g5, modsel, norm_g[l], hgrn_norm[l],
                           w_branch[l].astype(BF16), w_out[l].astype(BF16), nt, n_lat_tiles)
        xc = _mlp(x_new, h2, modsel, norm_g[l], w_mlp_up[l].astype(BF16), w_mlp_down[l].astype(BF16),
                  n_lat_tiles)
    return xc[:, :S]
```

```python
import functools
import math

import numpy as np
import jax
import jax.numpy as jnp
from jax import lax
from jax.experimental import pallas as pl
from jax.experimental.pallas import tpu as pltpu

F32 = jnp.float32
BF16 = jnp.bfloat16

D_MODEL = 1024
GRID_W = 64
CTX_LEN = 256
ROPE_BASE = 10000.0
ROPE_FREQS = 16
NORM_EPS = 1e-6
MASK_VALUE = -1e30
LOG2E = math.log2(math.e)

DIFF_HEADS = 4
DIFF_DIM = 64
HGRN_HEADS = 4
HGRN_DK = 128
SWA_KV_HEADS = 2
SWA_GROUP = 4
SWA_DIM = 64
WINDOW = 128
D_FF = 4 * D_MODEL

LANES = 128
TOKEN_TILE = 256
PROJ_MAX_TILE = 1024
HGRN_CHUNK = 64
HGRN_SUB = 8
ATT_TQ = 512
ATT_TK = 512
SWA_TQ = 256
VMEM_LIMIT = 56 * 1024 * 1024

_OFF = {}
_o = 0
for _n, _w in (("dq", 512), ("dk", 512), ("dv", 512), ("hq", 512), ("hff", 512), ("hfb", 512),
               ("hi", 512), ("hg", 512), ("sq", 512), ("sk", 128), ("sv", 128), ("gt", 3072)):
    _OFF[_n] = (_o, _o + _w)
    _o += _w


def _sigmoid(x):
    return 1.0 / (1.0 + jnp.exp(-x))


def _dot(a, b):
    return jnp.dot(a, b, preferred_element_type=F32)


def _dot_nt(a, b):
    return lax.dot_general(a, b, (((1,), (1,)), ((), ())), preferred_element_type=F32)


def _dot_tn(a, b):
    return lax.dot_general(a, b, (((0,), (0,)), ((), ())), preferred_element_type=F32)


def _rms(x, eps=NORM_EPS):
    return x * lax.rsqrt(jnp.mean(x * x, axis=-1, keepdims=True) + eps)


def _cparams(sem):
    return pltpu.CompilerParams(dimension_semantics=sem, vmem_limit_bytes=VMEM_LIMIT)


def _ada_kernel(c_ref, w_ref, b_ref, o_ref):
    c = c_ref[...]
    a = c * _sigmoid(c)
    o_ref[0] = _dot(a, w_ref[0]) + b_ref[0]


def _ada(cvec, w_ada, b_ada):
    depth, d, n = w_ada.shape
    tn = 1024
    return pl.pallas_call(
        _ada_kernel,
        grid=(depth, n // tn),
        in_specs=[pl.BlockSpec((8, d), lambda l, j: (0, 0)),
                  pl.BlockSpec((1, d, tn), lambda l, j: (l, 0, j)),
                  pl.BlockSpec((1, 1, tn), lambda l, j: (l, 0, j))],
        out_specs=pl.BlockSpec((1, 8, tn), lambda l, j: (l, 0, j)),
        out_shape=jax.ShapeDtypeStruct((depth, 8, n), F32),
        compiler_params=_cparams(("arbitrary", "arbitrary")),
        name="ada",
    )(cvec, w_ada, b_ada.reshape(depth, 1, n))


def _lb_kernel(p_ref, o_ref):
    p = p_ref[...]
    e = jnp.exp(p - jnp.max(p, axis=0, keepdims=True))
    sm = e / jnp.sum(e, axis=0, keepdims=True)
    run = jnp.zeros_like(sm[0])
    for l in range(p.shape[0]):
        run = run + sm[l]
        o_ref[l] = run - sm[0]


def _lower_bounds(hgrn_lb):
    return pl.pallas_call(
        _lb_kernel,
        out_shape=jax.ShapeDtypeStruct(hgrn_lb.shape, F32),
        name="hgrn_lower_bounds",
    )(hgrn_lb.astype(F32))


def _mod_slices(mod_ref, idx):
    return mod_ref[0, 0, :, idx * D_MODEL:(idx + 1) * D_MODEL]


def _norm_mod_kernel(x_ref, g_ref, mod_ref, o_ref):
    y = _rms(x_ref[0]) * g_ref[...]
    o_ref[0] = (y * (1.0 + _mod_slices(mod_ref, 1)) + _mod_slices(mod_ref, 0)).astype(o_ref.dtype)


def _mod_spec(n_lat_tiles):
    return pl.BlockSpec((1, 1, 1, 6 * D_MODEL), lambda b, i: (b, i // n_lat_tiles, 0, 0))


def _norm_mod(xc, g, modsel, n_lat_tiles):
    B, T, D = xc.shape
    return pl.pallas_call(
        _norm_mod_kernel,
        grid=(B, T // TOKEN_TILE),
        in_specs=[pl.BlockSpec((1, TOKEN_TILE, D), lambda b, i: (b, i, 0)),
                  pl.BlockSpec((1, D), lambda b, i: (0, 0)),
                  _mod_spec(n_lat_tiles)],
        out_specs=pl.BlockSpec((1, TOKEN_TILE, D), lambda b, i: (b, i, 0)),
        out_shape=jax.ShapeDtypeStruct((B, T, D), BF16),
        compiler_params=_cparams(("parallel", "parallel")),
        name="norm_mod",
    )(xc, g.reshape(1, D), modsel)


def _rope_partner(x):
    lane = lax.broadcasted_iota(jnp.int32, x.shape, 1)
    first = (lane % 32) < ROPE_FREQS
    return jnp.where(first, pltpu.roll(x, LANES - ROPE_FREQS, 1), pltpu.roll(x, ROPE_FREQS, 1))


def _proj_kernel(h_ref, w_ref, *rest, kind):
    acc = _dot(h_ref[0], w_ref[...])
    if kind == "plain":
        (o_ref,) = rest
        o_ref[0] = acc.astype(o_ref.dtype)
    elif kind == "silu":
        (o_ref,) = rest
        o_ref[0] = (acc * _sigmoid(acc)).astype(o_ref.dtype)
    elif kind == "sigmoid":
        (o_ref,) = rest
        o_ref[0] = _sigmoid(acc).astype(o_ref.dtype)
    elif kind == "rope":
        cos_ref, sin_ref, cscale_ref, o_ref = rest
        cos, sin = cos_ref[...], sin_ref[...]
        for j in range(acc.shape[1] // LANES):
            cols = slice(j * LANES, (j + 1) * LANES)
            a = acc[:, cols]
            o_ref[0, :, cols] = ((a * cos + _rope_partner(a) * sin) * cscale_ref[:, cols]).astype(o_ref.dtype)
    elif kind == "hgate":
        lb_ref, logf_ref, kk_ref = rest
        kk = (1.0 - lb_ref[...]) * _sigmoid(-acc)
        kk_ref[0] = kk
        logf_ref[0] = jnp.log1p(-kk)
    else:
        raise ValueError(kind)


def _proj(h, w, kind, tn, extra=(), out_dtype=BF16):
    B, T, D = h.shape
    n = w.shape[1]
    tm = max(t for t in range(TOKEN_TILE, PROJ_MAX_TILE + 1, TOKEN_TILE) if T % t == 0)
    assert n % tn == 0
    grid = (n // tn, B, T // tm)
    in_specs = [pl.BlockSpec((1, tm, D), lambda j, b, i: (b, i, 0)),
                pl.BlockSpec((D, tn), lambda j, b, i: (0, j))]
    out_spec = pl.BlockSpec((1, tm, tn), lambda j, b, i: (b, i, j))
    if kind == "rope":
        in_specs += [pl.BlockSpec((tm, LANES), lambda j, b, i: (i, 0))] * 2
        in_specs += [pl.BlockSpec((1, tn), lambda j, b, i: (0, j))]
        out_specs, out_shape = out_spec, jax.ShapeDtypeStruct((B, T, n), out_dtype)
    elif kind == "hgate":
        in_specs += [pl.BlockSpec((1, tn), lambda j, b, i: (0, j))]
        out_specs = [out_spec, out_spec]
        out_shape = [jax.ShapeDtypeStruct((B, T, n), F32)] * 2
    else:
        out_specs, out_shape = out_spec, jax.ShapeDtypeStruct((B, T, n), out_dtype)
    return pl.pallas_call(
        functools.partial(_proj_kernel, kind=kind),
        grid=grid, in_specs=in_specs, out_specs=out_specs, out_shape=out_shape,
        compiler_params=_cparams(("parallel", "parallel", "parallel")),
        name="proj_" + kind,
    )(h, w, *extra)


def _diff_attn_kernel(lam_ref, subln_ref, q_ref, k_ref, v_ref, o_ref, m_sc, acc_sc, s_sc, *,
                      lam_init, n_lat, tk, latent):
    q = q_ref[0]
    lane = lax.broadcasted_iota(jnp.int32, (1, LANES), 1)
    m_sc[...] = jnp.full(m_sc.shape, MASK_VALUE, F32)
    acc_sc[...] = jnp.zeros(acc_sc.shape, F32)

    def scores(start, size, c):
        k = k_ref[0, pl.ds(start, size), :]
        return _dot_nt(q, jnp.where((lane >= DIFF_DIM) == bool(c), k, jnp.zeros_like(k)))

    def accumulate(start, size, c, s):
        v = v_ref[0, pl.ds(start, size), :]
        v1 = jnp.concatenate([v, jnp.ones_like(v)], axis=-1)
        m_prev = m_sc[c]
        m_new = jnp.maximum(m_prev, jnp.max(s, axis=-1, keepdims=True))
        alpha = jnp.exp2(m_prev - m_new)
        p = jnp.exp2((s - m_new[:, :1]).astype(v.dtype))
        acc_sc[c] = jnp.concatenate([alpha, alpha], axis=-1) * acc_sc[c] + _dot(p, v1)
        m_sc[c] = m_new

    for c in range(2):
        accumulate(n_lat, CTX_LEN, c, scores(n_lat, CTX_LEN, c))
    if latent:
        n_tiles = n_lat // tk
        for c in range(2):
            s_sc[0, c] = scores(0, tk, c)

        def body(i, carry):
            for slot in range(2):
                j = 2 * i + slot
                nxt = pl.multiple_of(jnp.minimum(j + 1, n_tiles - 1) * tk, tk)
                cur = pl.multiple_of(j * tk, tk)
                for c in range(2):
                    s_sc[1 - slot, c] = scores(nxt, tk, c)
                    accumulate(cur, tk, c, s_sc[slot, c])
            return carry
        lax.fori_loop(0, n_tiles // 2, body, 0)

    lv = lam_ref[...]
    s01 = jnp.sum(lv[0:1] * lv[1:2], axis=-1, keepdims=True)
    s23 = jnp.sum(lv[2:3] * lv[3:4], axis=-1, keepdims=True)
    lam = jnp.exp(s01) - jnp.exp(s23) + lam_init
    a0, a1 = acc_sc[0], acc_sc[1]
    o = a0[:, :LANES] / a0[:, LANES:] - lam * (a1[:, :LANES] / a1[:, LANES:])
    o_ref[0] = (_rms(o) * subln_ref[...] * (1.0 - lam_init)).astype(o_ref.dtype)


def _diff_attention(g1, g2, lam_vecs, subln, lam_init, n_lat, latent):
    B, T, _ = g1.shape
    nh = DIFF_HEADS
    if latent:
        tq, nq, q_off, rows = ATT_TQ, n_lat // ATT_TQ, 0, n_lat
    else:
        tq, nq, q_off, rows = CTX_LEN, 1, n_lat // CTX_LEN, CTX_LEN
    in_specs = [pl.BlockSpec((4, DIFF_DIM), lambda b, h, i: (0, 0)),
                pl.BlockSpec((1, LANES), lambda b, h, i: (0, 0)),
                pl.BlockSpec((1, tq, LANES), lambda b, h, i: (b, q_off + i, h)),
                pl.BlockSpec((1, T, LANES), lambda b, h, i: (b, 0, nh + h)),
                pl.BlockSpec((1, T, LANES), lambda b, h, i: (b, 0, nh + h))]
    return pl.pallas_call(
        functools.partial(_diff_attn_kernel, lam_init=lam_init, n_lat=n_lat, tk=ATT_TK, latent=latent),
        grid=(B, nh, nq), in_specs=in_specs,
        out_specs=pl.BlockSpec((1, tq, LANES), lambda b, h, i: (b, i, h)),
        out_shape=jax.ShapeDtypeStruct((B, rows, nh * LANES), BF16),
        scratch_shapes=[pltpu.VMEM((2, tq, LANES), F32), pltpu.VMEM((2, tq, 2 * LANES), F32),
                        pltpu.VMEM((2, 2, tq, ATT_TK), F32)],
        compiler_params=_cparams(("parallel", "parallel", "arbitrary")),
        name="diff_attn_latent" if latent else "diff_attn_ctx",
    )(lam_vecs, subln.reshape(1, LANES), g1, g1, g2)


def _swa_kernel(sink_ref, q_ref, kc_ref, vc_ref, *rest, n_lat, latent):
    g = pl.program_id(1)
    if latent:
        kp_ref, kk_ref, kn_ref, vp_ref, vv_ref, vn_ref, o_ref = rest
        i = pl.program_id(2)
        k = jnp.concatenate([kc_ref[0], kp_ref[0], kk_ref[0], kn_ref[0]], axis=0)
        v = jnp.concatenate([vc_ref[0], vp_ref[0], vv_ref[0], vn_ref[0]], axis=0)
        tq = q_ref.shape[1]
        nkeys = k.shape[0]
        r = lax.broadcasted_iota(jnp.int32, (tq, nkeys), 0)
        j = lax.broadcasted_iota(jnp.int32, (tq, nkeys), 1) - CTX_LEN
        rel = j - WINDOW - r
        kpos = i * tq - WINDOW + j
        valid = (j < 0) | ((rel <= WINDOW) & (rel >= -WINDOW) & (kpos >= 0) & (kpos < n_lat))
    else:
        (o_ref,) = rest
        k, v = kc_ref[0], vc_ref[0]
        valid = None
    lane = lax.broadcasted_iota(jnp.int32, (1, LANES), 1)
    for pair in range(SWA_GROUP // 2):
        qp = q_ref[0, :, pair * LANES:(pair + 1) * LANES]
        out = None
        for e in range(2):
            half = (lane >= SWA_DIM) == bool(e)
            s = _dot_nt(qp, jnp.where(half, k, jnp.zeros_like(k)))
            if valid is not None:
                s = jnp.where(valid, s, MASK_VALUE)
            sink = sink_ref[g * SWA_GROUP + pair * 2 + e]
            m = jnp.maximum(jnp.max(s, axis=-1, keepdims=True), sink)
            p = jnp.exp(s - m)
            denom = jnp.sum(p, axis=-1, keepdims=True) + jnp.exp(sink - m)
            pv = _dot(p.astype(v.dtype), jnp.where(half, v, jnp.zeros_like(v))) / denom
            out = pv if out is None else out + pv
        o_ref[0, :, pair * LANES:(pair + 1) * LANES] = out.astype(o_ref.dtype)


def _swa(g1, g2, sink, n_lat, latent):
    B, T, _ = g1.shape
    ctx_blk = n_lat // CTX_LEN
    qcol = 1024 // (2 * LANES)
    kcol = 1536 // LANES
    vcol = 1024 // LANES
    smem = pl.BlockSpec(memory_space=pltpu.SMEM)
    if latent:
        tq = SWA_TQ
        half = tq // 2
        nq = n_lat // tq
        last_half = n_lat // half - 1
        grid = (B, SWA_KV_HEADS, nq)
        ctx_k = pl.BlockSpec((1, CTX_LEN, LANES), lambda b, g, i: (b, ctx_blk, kcol + g))
        ctx_v = pl.BlockSpec((1, CTX_LEN, LANES), lambda b, g, i: (b, ctx_blk, vcol + g))

        def win(col):
            return [pl.BlockSpec((1, half, LANES), lambda b, g, i: (b, jnp.maximum(2 * i - 1, 0), col + g)),
                    pl.BlockSpec((1, tq, LANES), lambda b, g, i: (b, i, col + g)),
                    pl.BlockSpec((1, half, LANES), lambda b, g, i: (b, jnp.minimum(2 * i + 2, last_half), col + g))]

        in_specs = [smem, pl.BlockSpec((1, tq, 2 * LANES), lambda b, g, i: (b, i, qcol + g)),
                    ctx_k, ctx_v] + win(kcol) + win(vcol)
        out_specs = pl.BlockSpec((1, tq, 2 * LANES), lambda b, g, i: (b, i, g))
        out_shape = jax.ShapeDtypeStruct((B, n_lat, 4 * LANES), BF16)
        args = (sink, g1, g1, g2, g1, g1, g1, g2, g2, g2)
        sem = ("parallel", "parallel", "parallel")
    else:
        grid = (B, SWA_KV_HEADS)
        in_specs = [smem, pl.BlockSpec((1, CTX_LEN, 2 * LANES), lambda b, g: (b, ctx_blk, qcol + g)),
                    pl.BlockSpec((1, CTX_LEN, LANES), lambda b, g: (b, ctx_blk, kcol + g)),
                    pl.BlockSpec((1, CTX_LEN, LANES), lambda b, g: (b, ctx_blk, vcol + g))]
        out_specs = pl.BlockSpec((1, CTX_LEN, 2 * LANES), lambda b, g: (b, 0, g))
        out_shape = jax.ShapeDtypeStruct((B, CTX_LEN, 4 * LANES), BF16)
        args = (sink, g1, g1, g2)
        sem = ("parallel", "parallel")
    return pl.pallas_call(
        functools.partial(_swa_kernel, n_lat=n_lat, latent=latent),
        grid=grid, in_specs=in_specs, out_specs=out_specs, out_shape=out_shape,
        compiler_params=_cparams(sem),
        name="swa_latent" if latent else "swa_ctx",
    )(*args)


def _hgrn_constants(reverse):
    C, sub = HGRN_CHUNK, HGRN_SUB
    t = np.arange(C)[:, None]
    u = np.arange(C)[None, :]
    blocks = [(u <= t)]
    masks = [((u // sub) == (t // sub)) & (u <= t)]
    h = sub
    while h < C:
        start = (t // (2 * h)) * (2 * h)
        anchor = start + h - 1
        is_q = ((t // h) % 2) == 1
        blocks.append(np.where(is_q, (u > anchor) & (u <= t), (u > t) & (u <= anchor)))
        masks.append(is_q & ((u // (2 * h)) == (t // (2 * h))) & (((u // h) % 2) == 0))
        h *= 2
    blocks.append(u > t)
    if reverse:
        blocks = [m[::-1, ::-1] for m in blocks]
        masks = [m[::-1, ::-1] for m in masks]
    cs = np.concatenate(blocks, axis=0).astype(np.float32)
    mk = np.concatenate(masks, axis=0).astype(np.float32)
    rows = np.arange(sub * HGRN_DK)[:, None] // HGRN_DK
    esel = (rows == (np.arange(C)[None, :] % sub)).astype(np.float32)
    return cs, mk, esel


def _hgrn_kernel(cs_ref, mk_ref, esel_ref, q_ref, lf_ref, kk_ref, v_ref, o_ref, st_sc, *, reverse):
    C, sub = HGRN_CHUNK, HGRN_SUB
    nlev = cs_ref.shape[0] // C - 2

    @pl.when(pl.program_id(1) == 0)
    def _():
        st_sc[...] = jnp.zeros(st_sc.shape, F32)

    cs = cs_ref[...]
    esel = esel_ref[...]
    masks = [mk_ref[n * C:(n + 1) * C, :] > 0.5 for n in range(nlev + 1)]
    for hh in range(HGRN_HEADS):
        sl = slice(hh * HGRN_DK, (hh + 1) * HGRN_DK)
        q = q_ref[0, :, sl].astype(F32)
        kk = kk_ref[0, :, sl]
        v = v_ref[0, :, sl]
        lf = lf_ref[0, :, sl]
        lf_hi = lf.astype(BF16)
        lf_lo = (lf - lf_hi.astype(F32)).astype(BF16)
        sums = _dot(cs, lf_hi) + _dot(cs, lf_lo)
        b = sums[0:C]
        q3, k3, b3 = (a.reshape(C // sub, sub, HGRN_DK) for a in (q, kk, b))
        tiles = []
        for sg in range(sub):
            d = jnp.minimum(b3 - b3[:, sg:sg + 1, :], 0.0)
            tiles.append((q3 * k3[:, sg:sg + 1, :] * jnp.exp(d)).reshape(C, HGRN_DK).astype(BF16))
        a_mat = jnp.where(masks[0], _dot(jnp.concatenate(tiles, axis=-1), esel), 0.0)
        for n in range(nlev):
            w = jnp.exp(sums[(n + 1) * C:(n + 2) * C])
            a_mat = a_mat + jnp.where(masks[n + 1],
                                      _dot_nt((q * w).astype(BF16), (kk * w).astype(BF16)), 0.0)
        st = st_sc[hh]
        o = _dot_nt((q * jnp.exp(b)).astype(BF16), st.astype(BF16)) + _dot(a_mat.astype(BF16), v)
        o_ref[0, :, sl] = o
        rem = sums[(nlev + 1) * C:(nlev + 2) * C]
        total = b[0:1] if reverse else b[C - 1:C]
        st_sc[hh] = st * jnp.exp(total) + _dot_tn(v, (kk * jnp.exp(rem)).astype(BF16))


def _hgrn_direction(g3, logf, kk, g2, n_lat, reverse):
    B, T, _ = g3.shape
    C = HGRN_CHUNK
    width = HGRN_HEADS * HGRN_DK
    nl, nc = n_lat // C, CTX_LEN // C
    cs, mk, esel = _hgrn_constants(reverse)
    d = 1 if reverse else 0

    if reverse:
        row = lambda i: nl + nc - 1 - i
    else:
        row = lambda i: jnp.where(i < nc, nl + i, i - nc)

    def tok(col):
        return pl.BlockSpec((1, C, width), lambda b, i: (b, row(i), col))

    const = lambda a: pl.BlockSpec(a.shape, lambda b, i: (0, 0))
    return pl.pallas_call(
        functools.partial(_hgrn_kernel, reverse=reverse),
        grid=(B, T // C),
        in_specs=[const(cs), const(mk), const(esel), tok(0), tok(d), tok(d), tok(0)],
        out_specs=tok(0),
        out_shape=jax.ShapeDtypeStruct((B, T, width), F32),
        scratch_shapes=[pltpu.VMEM((HGRN_HEADS, HGRN_DK, HGRN_DK), F32)],
        compiler_params=_cparams(("parallel", "arbitrary")),
        name="hgrn_bwd" if reverse else "hgrn_fwd",
    )(jnp.asarray(cs, BF16), jnp.asarray(mk, F32), jnp.asarray(esel, BF16), g3, logf, kk, g2)


def _merge_kernel(x_ref, ya_ref, of_ref, ob_ref, sg_ref, yc_ref, gt_ref, mod_ref, ng_ref, hn_ref,
                  wb_ref, wo_ref, xo_ref, h_ref):
    o = of_ref[0] + ob_ref[0]
    sg = sg_ref[0].astype(F32)
    yb = jnp.concatenate(
        [_rms(o[:, h * LANES:(h + 1) * LANES]) * hn_ref[...] * sg[:, h * LANES:(h + 1) * LANES]
         for h in range(HGRN_HEADS)], axis=-1).astype(BF16)
    ys = (ya_ref[0], yb, yc_ref[0])
    merged = None
    for j in range(3):
        gate = gt_ref[0, :, j * D_MODEL:(j + 1) * D_MODEL].astype(F32)
        term = gate * _dot(ys[j], wb_ref[j])
        merged = term if merged is None else merged + term
    y = _dot(merged.astype(BF16), wo_ref[...])
    x_new = x_ref[0] + _mod_slices(mod_ref, 2) * (_rms(y) * ng_ref[1:2])
    xo_ref[0] = x_new
    h2 = _rms(x_new) * ng_ref[2:3]
    h_ref[0] = (h2 * (1.0 + _mod_slices(mod_ref, 4)) + _mod_slices(mod_ref, 3)).astype(h_ref.dtype)


def _merge(xc, ya, o_f, o_b, g3, yc, g5, modsel, norm_g, hgrn_norm, wb, wo, n_tiles, n_lat_tiles):
    B, _, D = xc.shape
    tm = TOKEN_TILE
    rows = n_tiles * tm
    tokw = lambda w, col=0: pl.BlockSpec((1, tm, w), lambda b, i: (b, i, col))
    full = lambda a: pl.BlockSpec(a.shape, lambda b, i: (0,) * a.ndim)
    hn = hgrn_norm.reshape(1, LANES)
    return pl.pallas_call(
        _merge_kernel,
        grid=(B, n_tiles),
        in_specs=[tokw(D), tokw(512), tokw(512), tokw(512), tokw(512, 1), tokw(512), tokw(3 * D),
                  _mod_spec(n_lat_tiles), full(norm_g), full(hn), full(wb), full(wo)],
        out_specs=[tokw(D), tokw(D)],
        out_shape=[jax.ShapeDtypeStruct((B, rows, D), F32), jax.ShapeDtypeStruct((B, rows, D), BF16)],
        compiler_params=_cparams(("parallel", "parallel")),
        name="merge",
    )(xc, ya, o_f, o_b, g3, yc, g5, modsel, norm_g, hn, wb, wo)


def _mlp_kernel(x_ref, h_ref, mod_ref, ng_ref, wu_ref, wd_ref, o_ref):
    u = jnp.maximum(_dot(h_ref[0], wu_ref[...]), 0.0)
    y = _dot((u * u).astype(BF16), wd_ref[...])
    o_ref[0] = x_ref[0] + _mod_slices(mod_ref, 5) * (_rms(y) * ng_ref[3:4])


def _mlp(x_new, h2, modsel, norm_g, wu, wd, n_lat_tiles):
    B, rows, D = x_new.shape
    tm = TOKEN_TILE
    tok = pl.BlockSpec((1, tm, D), lambda b, i: (b, i, 0))
    full = lambda a: pl.BlockSpec(a.shape, lambda b, i: (0,) * a.ndim)
    return pl.pallas_call(
        _mlp_kernel,
        grid=(B, rows // tm),
        in_specs=[tok, tok, _mod_spec(n_lat_tiles), full(norm_g), full(wu), full(wd)],
        out_specs=tok,
        out_shape=jax.ShapeDtypeStruct((B, rows, D), F32),
        compiler_params=_cparams(("parallel", "parallel")),
        name="mlp",
    )(x_new, h2, modsel, norm_g, wu, wd)


def _rope_tables(n_lat):
    t = jnp.arange(n_lat)
    pos = jnp.stack([t // GRID_W, t % GRID_W], axis=-1).astype(F32)
    inv_freq = ROPE_BASE ** (-jnp.arange(ROPE_FREQS, dtype=F32) / ROPE_FREQS)
    ang = pos[:, :, None] * inv_freq
    cos, sin = jnp.cos(ang), jnp.sin(ang)
    cos64 = jnp.concatenate([cos, cos], axis=-1).reshape(n_lat, 4 * ROPE_FREQS)
    sin64 = jnp.concatenate([-sin, sin], axis=-1).reshape(n_lat, 4 * ROPE_FREQS)
    cos_t = jnp.concatenate([jnp.tile(cos64, (1, 2)), jnp.ones((CTX_LEN, LANES), F32)], axis=0)
    sin_t = jnp.concatenate([jnp.tile(sin64, (1, 2)), jnp.zeros((CTX_LEN, LANES), F32)], axis=0)
    return cos_t, sin_t


def _cols(w, *names):
    return [w[:, _OFF[n][0]:_OFF[n][1]] for n in names]


def _dup_heads(w):
    a, b = w[:, :SWA_DIM], w[:, SWA_DIM:]
    return jnp.concatenate([a, a, b, b], axis=1)


def kernel(x, c, ctx, c_ctx, w_ada, b_ada, norm_g, w_in, diff_lambda, diff_subln, hgrn_lb, hgrn_norm,
           swa_sink, w_branch, w_out, w_mlp_up, w_mlp_down):
    B, S, D = x.shape
    depth = w_ada.shape[0]
    assert D == D_MODEL and ctx.shape[1] == CTX_LEN and S % max(ATT_TQ, ATT_TK) == 0
    n_lat_tiles = S // TOKEN_TILE
    n_tiles = n_lat_tiles + 1

    cos_t, sin_t = _rope_tables(S)
    qscale = jnp.concatenate([jnp.full((1, 512), DIFF_DIM ** -0.5 * LOG2E, F32), jnp.ones((1, 512), F32),
                              jnp.full((1, 512), SWA_DIM ** -0.5, F32), jnp.ones((1, 256), F32)], axis=1)
    cvec = jnp.concatenate([c, c_ctx[None], jnp.zeros((8 - B - 1, D), F32)], axis=0)
    mods = _ada(cvec, w_ada, b_ada)
    lower = _lower_bounds(hgrn_lb)
    xc = jnp.concatenate([x, ctx], axis=1)

    for l in range(depth):
        need_ctx = l < depth - 1
        lam_init = 0.8 - 0.6 * math.exp(-0.3 * l)
        modsel = jnp.stack([mods[l, :B], jnp.broadcast_to(mods[l, B], (B, 6 * D))], axis=1)[:, :, None, :]
        wl = w_in[l]
        dq, dk, dv, hq, hff, hfb, hi, hg, sq, sk, sv, gt = _cols(
            wl, "dq", "dk", "dv", "hq", "hff", "hfb", "hi", "hg", "sq", "sk", "sv", "gt")
        w1 = jnp.concatenate([dq, dk, sq, _dup_heads(sk)], axis=1).astype(BF16)
        w2 = jnp.concatenate([hi, dv, _dup_heads(sv)], axis=1).astype(BF16)
        w3 = jnp.concatenate([hq, hg], axis=1).astype(BF16)
        w4 = jnp.concatenate([hff, hfb], axis=1).astype(BF16)
        w5 = gt.astype(BF16)

        h = _norm_mod(xc, norm_g[l, 0], modsel, n_lat_tiles)
        g1 = _proj(h, w1, "rope", 896, extra=(cos_t, sin_t, qscale))
        g2 = _proj(h, w2, "plain", 1280)
        g3 = _proj(h, w3, "silu", 1024)
        logf, kk = _proj(h, w4, "hgate", 1024, extra=(lower[l].reshape(1, 1024),))
        g5 = _proj(h, w5, "sigmoid", 1024)

        ya = _diff_attention(g1, g2, diff_lambda[l], diff_subln[l], lam_init, S, latent=True)
        yc = _swa(g1, g2, swa_sink[l], S, latent=True)
        if need_ctx:
            ya = jnp.concatenate(
                [ya, _diff_attention(g1, g2, diff_lambda[l], diff_subln[l], lam_init, S, latent=False)], axis=1)
            yc = jnp.concatenate([yc, _swa(g1, g2, swa_sink[l], S, latent=False)], axis=1)
        o_f = _hgrn_direction(g3, logf, kk, g2, S, reverse=False)
        o_b = _hgrn_direction(g3, logf, kk, g2, S, reverse=True)

        nt = n_tiles if need_ctx else n_lat_tiles
        x_new, h2 = _merge(xc, ya, o_f, o_b, g3, yc, g5, modsel, norm_g[l], hgrn_norm[l],
                           w_branch[l].astype(BF16), w_out[l].astype(BF16), nt, n_lat_tiles)
        xc = _mlp(x_new, h2, modsel, norm_g[l], w_mlp_up[l].astype(BF16), w_mlp_down[l].astype(BF16),
                  n_lat_tiles)
    return xc[:, :S]
```

```python
import functools
import math

import numpy as np
import jax
import jax.numpy as jnp
from jax import lax
from jax.experimental import pallas as pl
from jax.experimental.pallas import tpu as pltpu

F32 = jnp.float32
BF16 = jnp.bfloat16

D_MODEL = 1024
GRID_W = 64
CTX_LEN = 256
ROPE_BASE = 10000.0
ROPE_FREQS = 16
NORM_EPS = 1e-6
MASK_VALUE = -1e30
LOG2E = math.log2(math.e)

DIFF_HEADS = 4
DIFF_DIM = 64
HGRN_HEADS = 4
HGRN_DK = 128
SWA_KV_HEADS = 2
SWA_GROUP = 4
SWA_DIM = 64
WINDOW = 128
D_FF = 4 * D_MODEL

LANES = 128
TOKEN_TILE = 256
PROJ_MAX_TILE = 1024
HGRN_CHUNK = 64
HGRN_SUB = 8
ATT_TQ = 1024
ATT_TK = 512
ATT_UNROLL = 8
SWA_TQ = 256
VMEM_LIMIT = 56 * 1024 * 1024

_OFF = {}
_o = 0
for _n, _w in (("dq", 512), ("dk", 512), ("dv", 512), ("hq", 512), ("hff", 512), ("hfb", 512),
               ("hi", 512), ("hg", 512), ("sq", 512), ("sk", 128), ("sv", 128), ("gt", 3072)):
    _OFF[_n] = (_o, _o + _w)
    _o += _w


def _sigmoid(x):
    return 1.0 / (1.0 + jnp.exp(-x))


def _dot(a, b):
    return jnp.dot(a, b, preferred_element_type=F32)


def _dot_nt(a, b):
    return lax.dot_general(a, b, (((1,), (1,)), ((), ())), preferred_element_type=F32)


def _dot_tn(a, b):
    return lax.dot_general(a, b, (((0,), (0,)), ((), ())), preferred_element_type=F32)


def _rms(x, eps=NORM_EPS):
    return x * lax.rsqrt(jnp.mean(x * x, axis=-1, keepdims=True) + eps)


def _cparams(sem):
    return pltpu.CompilerParams(dimension_semantics=sem, vmem_limit_bytes=VMEM_LIMIT)


def _ada_kernel(c_ref, w_ref, b_ref, o_ref):
    c = c_ref[...]
    a = c * _sigmoid(c)
    o_ref[0] = _dot(a, w_ref[0]) + b_ref[0]


def _ada(cvec, w_ada, b_ada):
    depth, d, n = w_ada.shape
    tn = 1024
    return pl.pallas_call(
        _ada_kernel,
        grid=(depth, n // tn),
        in_specs=[pl.BlockSpec((8, d), lambda l, j: (0, 0)),
                  pl.BlockSpec((1, d, tn), lambda l, j: (l, 0, j)),
                  pl.BlockSpec((1, 1, tn), lambda l, j: (l, 0, j))],
        out_specs=pl.BlockSpec((1, 8, tn), lambda l, j: (l, 0, j)),
        out_shape=jax.ShapeDtypeStruct((depth, 8, n), F32),
        compiler_params=_cparams(("arbitrary", "arbitrary")),
        name="ada",
    )(cvec, w_ada, b_ada.reshape(depth, 1, n))


def _lb_kernel(p_ref, o_ref):
    p = p_ref[...]
    e = jnp.exp(p - jnp.max(p, axis=0, keepdims=True))
    sm = e / jnp.sum(e, axis=0, keepdims=True)
    run = jnp.zeros_like(sm[0])
    for l in range(p.shape[0]):
        run = run + sm[l]
        o_ref[l] = run - sm[0]


def _lower_bounds(hgrn_lb):
    return pl.pallas_call(
        _lb_kernel,
        out_shape=jax.ShapeDtypeStruct(hgrn_lb.shape, F32),
        name="hgrn_lower_bounds",
    )(hgrn_lb.astype(F32))


def _mod_slices(mod_ref, idx):
    return mod_ref[0, 0, :, idx * D_MODEL:(idx + 1) * D_MODEL]


def _norm_mod_kernel(x_ref, g_ref, mod_ref, o_ref):
    y = _rms(x_ref[0]) * g_ref[...]
    o_ref[0] = (y * (1.0 + _mod_slices(mod_ref, 1)) + _mod_slices(mod_ref, 0)).astype(o_ref.dtype)


def _mod_spec(n_lat_tiles):
    return pl.BlockSpec((1, 1, 1, 6 * D_MODEL), lambda b, i: (b, i // n_lat_tiles, 0, 0))


def _norm_mod(xc, g, modsel, n_lat_tiles):
    B, T, D = xc.shape
    return pl.pallas_call(
        _norm_mod_kernel,
        grid=(B, T // TOKEN_TILE),
        in_specs=[pl.BlockSpec((1, TOKEN_TILE, D), lambda b, i: (b, i, 0)),
                  pl.BlockSpec((1, D), lambda b, i: (0, 0)),
                  _mod_spec(n_lat_tiles)],
        out_specs=pl.BlockSpec((1, TOKEN_TILE, D), lambda b, i: (b, i, 0)),
        out_shape=jax.ShapeDtypeStruct((B, T, D), BF16),
        compiler_params=_cparams(("parallel", "parallel")),
        name="norm_mod",
    )(xc, g.reshape(1, D), modsel)


def _rope_partner(x):
    lane = lax.broadcasted_iota(jnp.int32, x.shape, 1)
    first = (lane % 32) < ROPE_FREQS
    return jnp.where(first, pltpu.roll(x, LANES - ROPE_FREQS, 1), pltpu.roll(x, ROPE_FREQS, 1))


def _proj_kernel(h_ref, w_ref, *rest, kind):
    acc = _dot(h_ref[0], w_ref[...])
    if kind == "plain":
        (o_ref,) = rest
        o_ref[0] = acc.astype(o_ref.dtype)
    elif kind == "silu":
        (o_ref,) = rest
        o_ref[0] = (acc * _sigmoid(acc)).astype(o_ref.dtype)
    elif kind == "sigmoid":
        (o_ref,) = rest
        o_ref[0] = _sigmoid(acc).astype(o_ref.dtype)
    elif kind == "rope":
        cos_ref, sin_ref, cscale_ref, o_ref = rest
        cos, sin = cos_ref[...], sin_ref[...]
        for j in range(acc.shape[1] // LANES):
            cols = slice(j * LANES, (j + 1) * LANES)
            a = acc[:, cols]
            o_ref[0, :, cols] = ((a * cos + _rope_partner(a) * sin) * cscale_ref[:, cols]).astype(o_ref.dtype)
    elif kind == "hgate":
        lb_ref, logf_ref, kk_ref = rest
        kk = (1.0 - lb_ref[...]) * _sigmoid(-acc)
        kk_ref[0] = kk
        logf_ref[0] = jnp.log1p(-kk)
    else:
        raise ValueError(kind)


def _proj(h, w, kind, tn, extra=(), out_dtype=BF16):
    B, T, D = h.shape
    n = w.shape[1]
    tm = max(t for t in range(TOKEN_TILE, PROJ_MAX_TILE + 1, TOKEN_TILE) if T % t == 0)
    assert n % tn == 0
    grid = (n // tn, B, T // tm)
    in_specs = [pl.BlockSpec((1, tm, D), lambda j, b, i: (b, i, 0)),
                pl.BlockSpec((D, tn), lambda j, b, i: (0, j))]
    out_spec = pl.BlockSpec((1, tm, tn), lambda j, b, i: (b, i, j))
    if kind == "rope":
        in_specs += [pl.BlockSpec((tm, LANES), lambda j, b, i: (i, 0))] * 2
        in_specs += [pl.BlockSpec((1, tn), lambda j, b, i: (0, j))]
        out_specs, out_shape = out_spec, jax.ShapeDtypeStruct((B, T, n), out_dtype)
    elif kind == "hgate":
        in_specs += [pl.BlockSpec((1, tn), lambda j, b, i: (0, j))]
        out_specs = [out_spec, out_spec]
        out_shape = [jax.ShapeDtypeStruct((B, T, n), F32)] * 2
    else:
        out_specs, out_shape = out_spec, jax.ShapeDtypeStruct((B, T, n), out_dtype)
    return pl.pallas_call(
        functools.partial(_proj_kernel, kind=kind),
        grid=grid, in_specs=in_specs, out_specs=out_specs, out_shape=out_shape,
        compiler_params=_cparams(("parallel", "parallel", "parallel")),
        name="proj_" + kind,
    )(h, w, *extra)


def _diff_attn_kernel(lam_ref, subln_ref, q_ref, k_ref, v_ref, o_ref, m_sc, acc_sc, s_sc, *,
                      lam_init, n_lat, tk, latent):
    q = q_ref[0]
    lane = lax.broadcasted_iota(jnp.int32, (1, LANES), 1)
    m_sc[...] = jnp.full(m_sc.shape, MASK_VALUE, F32)
    acc_sc[...] = jnp.zeros(acc_sc.shape, F32)

    def scores(start, size, c):
        k = k_ref[0, pl.ds(start, size), :]
        return _dot_nt(q, jnp.where((lane >= DIFF_DIM) == bool(c), k, jnp.zeros_like(k)))

    def accumulate(start, size, c, s):
        v = v_ref[0, pl.ds(start, size), :]
        v1 = jnp.concatenate([v, jnp.ones_like(v)], axis=-1)
        m_prev = m_sc[c]
        m_new = jnp.maximum(m_prev, jnp.max(s, axis=-1, keepdims=True))
        alpha = jnp.exp2(m_prev - m_new)
        p = jnp.exp2((s - m_new[:, :1]).astype(v.dtype))
        acc_sc[c] = jnp.concatenate([alpha, alpha], axis=-1) * acc_sc[c] + _dot(p, v1)
        m_sc[c] = m_new

    for c in range(2):
        accumulate(n_lat, CTX_LEN, c, scores(n_lat, CTX_LEN, c))
    if latent:
        n_tiles = n_lat // tk
        for c in range(2):
            s_sc[0, c] = scores(0, tk, c)

        unroll = min(ATT_UNROLL, n_tiles)

        def body(i, carry):
            for u in range(unroll):
                j, slot = unroll * i + u, u % 2
                nxt = pl.multiple_of(jnp.minimum(j + 1, n_tiles - 1) * tk, tk)
                cur = pl.multiple_of(j * tk, tk)
                for c in range(2):
                    s_sc[1 - slot, c] = scores(nxt, tk, c)
                    accumulate(cur, tk, c, s_sc[slot, c])
            return carry
        assert unroll % 2 == 0 and n_tiles % unroll == 0
        lax.fori_loop(0, n_tiles // unroll, body, 0)

    lv = lam_ref[...]
    s01 = jnp.sum(lv[0:1] * lv[1:2], axis=-1, keepdims=True)
    s23 = jnp.sum(lv[2:3] * lv[3:4], axis=-1, keepdims=True)
    lam = jnp.exp(s01) - jnp.exp(s23) + lam_init
    a0, a1 = acc_sc[0], acc_sc[1]
    o = a0[:, :LANES] / a0[:, LANES:] - lam * (a1[:, :LANES] / a1[:, LANES:])
    o_ref[0] = (_rms(o) * subln_ref[...] * (1.0 - lam_init)).astype(o_ref.dtype)


def _diff_attention(g1, g2, lam_vecs, subln, lam_init, n_lat, latent):
    B, T, _ = g1.shape
    nh = DIFF_HEADS
    if latent:
        tq, nq, q_off, rows = ATT_TQ, n_lat // ATT_TQ, 0, n_lat
    else:
        tq, nq, q_off, rows = CTX_LEN, 1, n_lat // CTX_LEN, CTX_LEN
    in_specs = [pl.BlockSpec((4, DIFF_DIM), lambda b, h, i: (0, 0)),
                pl.BlockSpec((1, LANES), lambda b, h, i: (0, 0)),
                pl.BlockSpec((1, tq, LANES), lambda b, h, i: (b, q_off + i, h)),
                pl.BlockSpec((1, T, LANES), lambda b, h, i: (b, 0, nh + h)),
                pl.BlockSpec((1, T, LANES), lambda b, h, i: (b, 0, nh + h))]
    return pl.pallas_call(
        functools.partial(_diff_attn_kernel, lam_init=lam_init, n_lat=n_lat, tk=ATT_TK, latent=latent),
        grid=(B, nh, nq), in_specs=in_specs,
        out_specs=pl.BlockSpec((1, tq, LANES), lambda b, h, i: (b, i, h)),
        out_shape=jax.ShapeDtypeStruct((B, rows, nh * LANES), BF16),
        scratch_shapes=[pltpu.VMEM((2, tq, LANES), F32), pltpu.VMEM((2, tq, 2 * LANES), F32),
                        pltpu.VMEM((2, 2, tq, ATT_TK), F32)],
        compiler_params=_cparams(("parallel", "parallel", "arbitrary")),
        name="diff_attn_latent" if latent else "diff_attn_ctx",
    )(lam_vecs, subln.reshape(1, LANES), g1, g1, g2)


def _swa_bias(tq):
    half = tq // 2
    assert half == WINDOW
    r = np.arange(tq)[:, None]
    wj = np.arange(2 * tq)[None, :]
    band = np.abs(wj - half - r) <= WINDOW
    variants = [band & (wj >= half), band, band & (wj < tq + half)]
    out = []
    for keep in variants:
        keep = np.concatenate([np.ones((tq, CTX_LEN), bool), keep], axis=1)
        out.append(np.where(keep, 0.0, MASK_VALUE).astype(np.float32))
    return np.stack(out)


def _swa_kernel(sink_ref, q_ref, kc_ref, vc_ref, *rest, latent):
    g = pl.program_id(1)
    if latent:
        bias_ref, kp_ref, kk_ref, kn_ref, vp_ref, vv_ref, vn_ref, o_ref = rest
        k = jnp.concatenate([kc_ref[0], kp_ref[0], kk_ref[0], kn_ref[0]], axis=0)
        v = jnp.concatenate([vc_ref[0], vp_ref[0], vv_ref[0], vn_ref[0]], axis=0)
        bias = bias_ref[0]
    else:
        (o_ref,) = rest
        k, v = kc_ref[0], vc_ref[0]
        bias = None
    lane = lax.broadcasted_iota(jnp.int32, (1, LANES), 1)
    ones = jnp.ones_like(v)
    for pair in range(SWA_GROUP // 2):
        qp = q_ref[0, :, pair * LANES:(pair + 1) * LANES]
        out = None
        for e in range(2):
            half = (lane >= SWA_DIM) == bool(e)
            s = _dot_nt(qp, jnp.where(half, k, jnp.zeros_like(k)))
            if bias is not None:
                s = s + bias
            sink = sink_ref[g * SWA_GROUP + pair * 2 + e] * LOG2E
            m = jnp.maximum(jnp.max(s, axis=-1, keepdims=True), sink)
            p = jnp.exp2((s - m).astype(v.dtype))
            v1 = jnp.concatenate([jnp.where(half, v, jnp.zeros_like(v)), ones], axis=-1)
            pv = _dot(p, v1)
            pv = pv[:, :LANES] / (pv[:, LANES:] + jnp.exp2(sink - m))
            out = pv if out is None else out + pv
        o_ref[0, :, pair * LANES:(pair + 1) * LANES] = out.astype(o_ref.dtype)


def _swa(g1, g2, sink, n_lat, latent):
    B, T, _ = g1.shape
    ctx_blk = n_lat // CTX_LEN
    qcol = 1024 // (2 * LANES)
    kcol = 1536 // LANES
    vcol = 1024 // LANES
    smem = pl.BlockSpec(memory_space=pltpu.SMEM)
    if latent:
        tq = SWA_TQ
        half = tq // 2
        nq = n_lat // tq
        last_half = n_lat // half - 1
        grid = (B, SWA_KV_HEADS, nq)
        ctx_k = pl.BlockSpec((1, CTX_LEN, LANES), lambda b, g, i: (b, ctx_blk, kcol + g))
        ctx_v = pl.BlockSpec((1, CTX_LEN, LANES), lambda b, g, i: (b, ctx_blk, vcol + g))

        def win(col):
            return [pl.BlockSpec((1, half, LANES), lambda b, g, i: (b, jnp.maximum(2 * i - 1, 0), col + g)),
                    pl.BlockSpec((1, tq, LANES), lambda b, g, i: (b, i, col + g)),
                    pl.BlockSpec((1, half, LANES), lambda b, g, i: (b, jnp.minimum(2 * i + 2, last_half), col + g))]

        assert nq >= 2
        bias = _swa_bias(tq)
        bias_spec = pl.BlockSpec((1,) + bias.shape[1:],
                                 lambda b, g, i: (jnp.where(i == 0, 0, jnp.where(i == nq - 1, 2, 1)), 0, 0))
        in_specs = [smem, pl.BlockSpec((1, tq, 2 * LANES), lambda b, g, i: (b, i, qcol + g)),
                    ctx_k, ctx_v, bias_spec] + win(kcol) + win(vcol)
        out_specs = pl.BlockSpec((1, tq, 2 * LANES), lambda b, g, i: (b, i, g))
        out_shape = jax.ShapeDtypeStruct((B, n_lat, 4 * LANES), BF16)
        args = (sink, g1, g1, g2, jnp.asarray(bias), g1, g1, g1, g2, g2, g2)
        sem = ("parallel", "parallel", "parallel")
    else:
        grid = (B, SWA_KV_HEADS)
        in_specs = [smem, pl.BlockSpec((1, CTX_LEN, 2 * LANES), lambda b, g: (b, ctx_blk, qcol + g)),
                    pl.BlockSpec((1, CTX_LEN, LANES), lambda b, g: (b, ctx_blk, kcol + g)),
                    pl.BlockSpec((1, CTX_LEN, LANES), lambda b, g: (b, ctx_blk, vcol + g))]
        out_specs = pl.BlockSpec((1, CTX_LEN, 2 * LANES), lambda b, g: (b, 0, g))
        out_shape = jax.ShapeDtypeStruct((B, CTX_LEN, 4 * LANES), BF16)
        args = (sink, g1, g1, g2)
        sem = ("parallel", "parallel")
    return pl.pallas_call(
        functools.partial(_swa_kernel, latent=latent),
        grid=grid, in_specs=in_specs, out_specs=out_specs, out_shape=out_shape,
        compiler_params=_cparams(sem),
        name="swa_latent" if latent else "swa_ctx",
    )(*args)


def _hgrn_levels():
    h, out = HGRN_SUB, []
    while h < HGRN_CHUNK:
        out.append(h)
        h *= 2
    return out


def _hgrn_constants():
    C, sub = HGRN_CHUNK, HGRN_SUB
    t = np.arange(C)[:, None]
    u = np.arange(C)[None, :]
    masks = [((u // sub) == (t // sub)) & (u <= t)]
    for h in _hgrn_levels():
        masks.append((((t // h) % 2) == 1) & ((u // (2 * h)) == (t // (2 * h))) & (((u // h) % 2) == 0))
    fwd = np.concatenate(masks, axis=0)
    bwd = np.concatenate([m[::-1, ::-1] for m in masks], axis=0)
    mk = np.stack([fwd, bwd]).astype(np.float32)
    rows = np.arange(sub * HGRN_DK)[:, None] // HGRN_DK
    esel = (rows == (np.arange(C)[None, :] % sub)).astype(np.float32)
    return mk, esel


def _chunk_cumsum(x, reverse):
    n = x.shape[0]
    row = lax.broadcasted_iota(jnp.int32, x.shape, 0)
    d = 1
    while d < n:
        if reverse:
            x = x + jnp.where(row < n - d, pltpu.roll(x, n - d, 0), 0.0)
        else:
            x = x + jnp.where(row >= d, pltpu.roll(x, d, 0), 0.0)
        d *= 2
    return x


def _hgrn_chunk(q, kk, lf, v, st, masks, esel, reverse):
    C, sub, dk = HGRN_CHUNK, HGRN_SUB, HGRN_DK
    b = _chunk_cumsum(lf * LOG2E, reverse)
    q3, k3, b3 = (a.reshape(C // sub, sub, dk) for a in (q, kk, b))
    tiles = []
    for sg in range(sub):
        d = jnp.minimum(b3 - b3[:, sg:sg + 1, :], 0.0)
        tiles.append((q3 * k3[:, sg:sg + 1, :] * jnp.exp2(d)).reshape(C, dk).astype(BF16))
    a_mat = jnp.where(masks[0], _dot(jnp.concatenate(tiles, axis=-1), esel), 0.0)
    for n, h in enumerate(_hgrn_levels()):
        bh = b.reshape(C // (2 * h), 2 * h, dk)
        r = lax.broadcasted_iota(jnp.int32, bh.shape, 1)
        if reverse:
            anchor, is_query = bh[:, h:h + 1, :], r < h
        else:
            anchor, is_query = bh[:, h - 1:h, :], r >= h
        w = jnp.exp2(jnp.where(is_query, bh - anchor, anchor - bh)).reshape(C, dk)
        a_mat = a_mat + jnp.where(masks[n + 1], _dot_nt((q * w).astype(BF16), (kk * w).astype(BF16)), 0.0)
    o = _dot_nt((q * jnp.exp2(b)).astype(BF16), st.astype(BF16)) + _dot(a_mat.astype(BF16), v)
    total = b[0:1] if reverse else b[C - 1:C]
    st_new = st * jnp.exp2(total) + _dot_tn(v, (kk * jnp.exp2(total - b)).astype(BF16))
    return o, st_new


def _hgrn_kernel(mk_ref, esel_ref, qf_ref, lff_ref, kkf_ref, vf_ref, qb_ref, lfb_ref, kkb_ref, vb_ref,
                 of_ref, ob_ref, st_sc):
    C = HGRN_CHUNK
    nmask = len(_hgrn_levels()) + 1

    @pl.when(pl.program_id(1) == 0)
    def _():
        st_sc[...] = jnp.zeros(st_sc.shape, F32)

    esel = esel_ref[...]
    dirs = ((qf_ref, lff_ref, kkf_ref, vf_ref, of_ref), (qb_ref, lfb_ref, kkb_ref, vb_ref, ob_ref))
    for d, (q_ref, lf_ref, kk_ref, v_ref, o_ref) in enumerate(dirs):
        masks = [mk_ref[d, n * C:(n + 1) * C, :] > 0.5 for n in range(nmask)]
        for hh in range(HGRN_HEADS):
            sl = slice(hh * HGRN_DK, (hh + 1) * HGRN_DK)
            o, st = _hgrn_chunk(q_ref[0, :, sl].astype(F32), kk_ref[0, :, sl], lf_ref[0, :, sl],
                                v_ref[0, :, sl], st_sc[d, hh], masks, esel, reverse=bool(d))
            o_ref[0, :, sl] = o
            st_sc[d, hh] = st


def _hgrn(g3, logf, kk, g2, n_lat):
    B, T, _ = g3.shape
    C = HGRN_CHUNK
    width = HGRN_HEADS * HGRN_DK
    nl, nc = n_lat // C, CTX_LEN // C
    mk, esel = _hgrn_constants()
    row_f = lambda i: jnp.where(i < nc, nl + i, i - nc)
    row_b = lambda i: nl + nc - 1 - i

    def tok(row, col):
        return pl.BlockSpec((1, C, width), lambda b, i: (b, row(i), col))

    const = lambda a: pl.BlockSpec(a.shape, lambda b, i: (0,) * a.ndim)
    out = jax.ShapeDtypeStruct((B, T, width), F32)
    return pl.pallas_call(
        _hgrn_kernel,
        grid=(B, T // C),
        in_specs=[const(mk), const(esel),
                  tok(row_f, 0), tok(row_f, 0), tok(row_f, 0), tok(row_f, 0),
                  tok(row_b, 0), tok(row_b, 1), tok(row_b, 1), tok(row_b, 0)],
        out_specs=[tok(row_f, 0), tok(row_b, 0)],
        out_shape=[out, out],
        scratch_shapes=[pltpu.VMEM((2, HGRN_HEADS, HGRN_DK, HGRN_DK), F32)],
        compiler_params=_cparams(("parallel", "arbitrary")),
        name="hgrn",
    )(jnp.asarray(mk, F32), jnp.asarray(esel, BF16), g3, logf, kk, g2, g3, logf, kk, g2)


def _merge_kernel(x_ref, ya_ref, of_ref, ob_ref, sg_ref, yc_ref, gt_ref, mod_ref, ng_ref, hn_ref,
                  wb_ref, wo_ref, xo_ref, h_ref):
    o = of_ref[0] + ob_ref[0]
    sg = sg_ref[0].astype(F32)
    yb = jnp.concatenate(
        [_rms(o[:, h * LANES:(h + 1) * LANES]) * hn_ref[...] * sg[:, h * LANES:(h + 1) * LANES]
         for h in range(HGRN_HEADS)], axis=-1).astype(BF16)
    ys = (ya_ref[0], yb, yc_ref[0])
    merged = None
    for j in range(3):
        gate = gt_ref[0, :, j * D_MODEL:(j + 1) * D_MODEL].astype(F32)
        term = gate * _dot(ys[j], wb_ref[j])
        merged = term if merged is None else merged + term
    y = _dot(merged.astype(BF16), wo_ref[...])
    x_new = x_ref[0] + _mod_slices(mod_ref, 2) * (_rms(y) * ng_ref[1:2])
    xo_ref[0] = x_new
    h2 = _rms(x_new) * ng_ref[2:3]
    h_ref[0] = (h2 * (1.0 + _mod_slices(mod_ref, 4)) + _mod_slices(mod_ref, 3)).astype(h_ref.dtype)


def _merge(xc, ya, o_f, o_b, g3, yc, g5, modsel, norm_g, hgrn_norm, wb, wo, n_tiles, n_lat_tiles):
    B, _, D = xc.shape
    tm = TOKEN_TILE
    rows = n_tiles * tm
    tokw = lambda w, col=0: pl.BlockSpec((1, tm, w), lambda b, i: (b, i, col))
    full = lambda a: pl.BlockSpec(a.shape, lambda b, i: (0,) * a.ndim)
    hn = hgrn_norm.reshape(1, LANES)
    return pl.pallas_call(
        _merge_kernel,
        grid=(B, n_tiles),
        in_specs=[tokw(D), tokw(512), tokw(512), tokw(512), tokw(512, 1), tokw(512), tokw(3 * D),
                  _mod_spec(n_lat_tiles), full(norm_g), full(hn), full(wb), full(wo)],
        out_specs=[tokw(D), tokw(D)],
        out_shape=[jax.ShapeDtypeStruct((B, rows, D), F32), jax.ShapeDtypeStruct((B, rows, D), BF16)],
        compiler_params=_cparams(("parallel", "parallel")),
        name="merge",
    )(xc, ya, o_f, o_b, g3, yc, g5, modsel, norm_g, hn, wb, wo)


def _mlp_kernel(x_ref, h_ref, mod_ref, ng_ref, wu_ref, wd_ref, o_ref):
    u = jnp.maximum(_dot(h_ref[0], wu_ref[...]), 0.0)
    y = _dot((u * u).astype(BF16), wd_ref[...])
    o_ref[0] = x_ref[0] + _mod_slices(mod_ref, 5) * (_rms(y) * ng_ref[3:4])


def _mlp(x_new, h2, modsel, norm_g, wu, wd, n_lat_tiles):
    B, rows, D = x_new.shape
    tm = TOKEN_TILE
    tok = pl.BlockSpec((1, tm, D), lambda b, i: (b, i, 0))
    full = lambda a: pl.BlockSpec(a.shape, lambda b, i: (0,) * a.ndim)
    return pl.pallas_call(
        _mlp_kernel,
        grid=(B, rows // tm),
        in_specs=[tok, tok, _mod_spec(n_lat_tiles), full(norm_g), full(wu), full(wd)],
        out_specs=tok,
        out_shape=jax.ShapeDtypeStruct((B, rows, D), F32),
        compiler_params=_cparams(("parallel", "parallel")),
        name="mlp",
    )(x_new, h2, modsel, norm_g, wu, wd)


def _rope_tables(n_lat):
    t = jnp.arange(n_lat)
    pos = jnp.stack([t // GRID_W, t % GRID_W], axis=-1).astype(F32)
    inv_freq = ROPE_BASE ** (-jnp.arange(ROPE_FREQS, dtype=F32) / ROPE_FREQS)
    ang = pos[:, :, None] * inv_freq
    cos, sin = jnp.cos(ang), jnp.sin(ang)
    cos64 = jnp.concatenate([cos, cos], axis=-1).reshape(n_lat, 4 * ROPE_FREQS)
    sin64 = jnp.concatenate([-sin, sin], axis=-1).reshape(n_lat, 4 * ROPE_FREQS)
    cos_t = jnp.concatenate([jnp.tile(cos64, (1, 2)), jnp.ones((CTX_LEN, LANES), F32)], axis=0)
    sin_t = jnp.concatenate([jnp.tile(sin64, (1, 2)), jnp.zeros((CTX_LEN, LANES), F32)], axis=0)
    return cos_t, sin_t


def _cols(w, *names):
    return [w[:, _OFF[n][0]:_OFF[n][1]] for n in names]


def _dup_heads(w):
    a, b = w[:, :SWA_DIM], w[:, SWA_DIM:]
    return jnp.concatenate([a, a, b, b], axis=1)


def kernel(x, c, ctx, c_ctx, w_ada, b_ada, norm_g, w_in, diff_lambda, diff_subln, hgrn_lb, hgrn_norm,
           swa_sink, w_branch, w_out, w_mlp_up, w_mlp_down):
    B, S, D = x.shape
    depth = w_ada.shape[0]
    assert D == D_MODEL and ctx.shape[1] == CTX_LEN and S % max(ATT_TQ, ATT_TK) == 0
    n_lat_tiles = S // TOKEN_TILE
    n_tiles = n_lat_tiles + 1

    cos_t, sin_t = _rope_tables(S)
    qscale = jnp.concatenate([jnp.full((1, 512), DIFF_DIM ** -0.5 * LOG2E, F32), jnp.ones((1, 512), F32),
                              jnp.full((1, 512), SWA_DIM ** -0.5 * LOG2E, F32), jnp.ones((1, 256), F32)], axis=1)
    cvec = jnp.concatenate([c, c_ctx[None], jnp.zeros((8 - B - 1, D), F32)], axis=0)
    mods = _ada(cvec, w_ada, b_ada)
    lower = _lower_bounds(hgrn_lb)
    xc = jnp.concatenate([x, ctx], axis=1)

    for l in range(depth):
        need_ctx = l < depth - 1
        lam_init = 0.8 - 0.6 * math.exp(-0.3 * l)
        modsel = jnp.stack([mods[l, :B], jnp.broadcast_to(mods[l, B], (B, 6 * D))], axis=1)[:, :, None, :]
        wl = w_in[l]
        dq, dk, dv, hq, hff, hfb, hi, hg, sq, sk, sv, gt = _cols(
            wl, "dq", "dk", "dv", "hq", "hff", "hfb", "hi", "hg", "sq", "sk", "sv", "gt")
        w1 = jnp.concatenate([dq, dk, sq, _dup_heads(sk)], axis=1).astype(BF16)
        w2 = jnp.concatenate([hi, dv, _dup_heads(sv)], axis=1).astype(BF16)
        w3 = jnp.concatenate([hq, hg], axis=1).astype(BF16)
        w4 = jnp.concatenate([hff, hfb], axis=1).astype(BF16)
        w5 = gt.astype(BF16)

        h = _norm_mod(xc, norm_g[l, 0], modsel, n_lat_tiles)
        g1 = _proj(h, w1, "rope", 896, extra=(cos_t, sin_t, qscale))
        g2 = _proj(h, w2, "plain", 1280)
        g3 = _proj(h, w3, "silu", 1024)
        logf, kk = _proj(h, w4, "hgate", 1024, extra=(lower[l].reshape(1, 1024),))
        g5 = _proj(h, w5, "sigmoid", 1024)

        ya = _diff_attention(g1, g2, diff_lambda[l], diff_subln[l], lam_init, S, latent=True)
        yc = _swa(g1, g2, swa_sink[l], S, latent=True)
        if need_ctx:
            ya = jnp.concatenate(
                [ya, _diff_attention(g1, g2, diff_lambda[l], diff_subln[l], lam_init, S, latent=False)], axis=1)
            yc = jnp.concatenate([yc, _swa(g1, g2, swa_sink[l], S, latent=False)], axis=1)
        o_f, o_b = _hgrn(g3, logf, kk, g2, S)

        nt = n_tiles if need_ctx else n_lat_tiles
        x_new, h2 = _merge(xc, ya, o_f, o_b, g3, yc, g5, modsel, norm_g[l], hgrn_norm[l],
                           w_branch[l].astype(BF16), w_out[l].astype(BF16), nt, n_lat_tiles)
        xc = _mlp(x_new, h2, modsel, norm_g[l], w_mlp_up[l].astype(BF16), w_mlp_down[l].astype(BF16),
                  n_lat_tiles)
    return xc[:, :S]
```

```python
import functools
import math

import numpy as np
import jax
import jax.numpy as jnp
from jax import lax
from jax.experimental import pallas as pl
from jax.experimental.pallas import tpu as pltpu

F32 = jnp.float32
BF16 = jnp.bfloat16

D_MODEL = 1024
GRID_W = 64
CTX_LEN = 256
ROPE_BASE = 10000.0
ROPE_FREQS = 16
NORM_EPS = 1e-6
MASK_VALUE = -1e30
LOG2E = math.log2(math.e)

DIFF_HEADS = 4
DIFF_DIM = 64
HGRN_HEADS = 4
HGRN_DK = 128
SWA_KV_HEADS = 2
SWA_GROUP = 4
SWA_DIM = 64
WINDOW = 128
D_FF = 4 * D_MODEL

LANES = 128
TOKEN_TILE = 512
PROJ_MAX_TILE = 1024
HGRN_CHUNK = 64
HGRN_SUB = 8
HGRN_STEP_CHUNKS = 4
ATT_TQ = 1024
ATT_TK = 768
ATT_ROWS = 512
SWA_TQ = 256
VMEM_LIMIT = 56 * 1024 * 1024

_OFF = {}
_o = 0
for _n, _w in (("dq", 512), ("dk", 512), ("dv", 512), ("hq", 512), ("hff", 512), ("hfb", 512),
               ("hi", 512), ("hg", 512), ("sq", 512), ("sk", 128), ("sv", 128), ("gt", 3072)):
    _OFF[_n] = (_o, _o + _w)
    _o += _w


def _sigmoid(x):
    return 1.0 / (1.0 + jnp.exp(-x))


def _dot(a, b):
    return jnp.dot(a, b, preferred_element_type=F32)


def _dot_nt(a, b):
    return lax.dot_general(a, b, (((1,), (1,)), ((), ())), preferred_element_type=F32)


def _dot_tn(a, b):
    return lax.dot_general(a, b, (((0,), (0,)), ((), ())), preferred_element_type=F32)


def _rms(x, eps=NORM_EPS):
    return x * lax.rsqrt(jnp.mean(x * x, axis=-1, keepdims=True) + eps)


def _cparams(sem):
    return pltpu.CompilerParams(dimension_semantics=sem, vmem_limit_bytes=VMEM_LIMIT)


def _ada_kernel(c_ref, w_ref, b_ref, o_ref):
    c = c_ref[...]
    a = c * _sigmoid(c)
    o_ref[0] = _dot(a, w_ref[0]) + b_ref[0]


def _ada(cvec, w_ada, b_ada):
    depth, d, n = w_ada.shape
    tn = 1024
    return pl.pallas_call(
        _ada_kernel,
        grid=(depth, n // tn),
        in_specs=[pl.BlockSpec((8, d), lambda l, j: (0, 0)),
                  pl.BlockSpec((1, d, tn), lambda l, j: (l, 0, j)),
                  pl.BlockSpec((1, 1, tn), lambda l, j: (l, 0, j))],
        out_specs=pl.BlockSpec((1, 8, tn), lambda l, j: (l, 0, j)),
        out_shape=jax.ShapeDtypeStruct((depth, 8, n), F32),
        compiler_params=_cparams(("arbitrary", "arbitrary")),
        name="ada",
    )(cvec, w_ada, b_ada.reshape(depth, 1, n))


def _lb_kernel(p_ref, o_ref):
    p = p_ref[...]
    e = jnp.exp(p - jnp.max(p, axis=0, keepdims=True))
    sm = e / jnp.sum(e, axis=0, keepdims=True)
    run = jnp.zeros_like(sm[0])
    for l in range(p.shape[0]):
        run = run + sm[l]
        o_ref[l] = run - sm[0]


def _lower_bounds(hgrn_lb):
    return pl.pallas_call(
        _lb_kernel,
        out_shape=jax.ShapeDtypeStruct(hgrn_lb.shape, F32),
        name="hgrn_lower_bounds",
    )(hgrn_lb.astype(F32))


def _mod_slices(mod_ref, idx):
    return mod_ref[0, 0, :, idx * D_MODEL:(idx + 1) * D_MODEL]


def _norm_mod_kernel(x_ref, g_ref, mod_ref, o_ref):
    y = _rms(x_ref[0]) * g_ref[...]
    o_ref[0] = (y * (1.0 + _mod_slices(mod_ref, 1)) + _mod_slices(mod_ref, 0)).astype(o_ref.dtype)


def _mod_spec(n_lat_tiles):
    return pl.BlockSpec((1, 1, 1, 6 * D_MODEL), lambda b, i: (b, i // n_lat_tiles, 0, 0))


def _norm_mod(xc, g, modsel, n_lat_tiles):
    B, T, D = xc.shape
    return pl.pallas_call(
        _norm_mod_kernel,
        grid=(B, pl.cdiv(T, TOKEN_TILE)),
        in_specs=[pl.BlockSpec((1, TOKEN_TILE, D), lambda b, i: (b, i, 0)),
                  pl.BlockSpec((1, D), lambda b, i: (0, 0)),
                  _mod_spec(n_lat_tiles)],
        out_specs=pl.BlockSpec((1, TOKEN_TILE, D), lambda b, i: (b, i, 0)),
        out_shape=jax.ShapeDtypeStruct((B, T, D), BF16),
        compiler_params=_cparams(("parallel", "parallel")),
        name="norm_mod",
    )(xc, g.reshape(1, D), modsel)


def _rope_partner(x):
    lane = lax.broadcasted_iota(jnp.int32, x.shape, 1)
    first = (lane % 32) < ROPE_FREQS
    return jnp.where(first, pltpu.roll(x, LANES - ROPE_FREQS, 1), pltpu.roll(x, ROPE_FREQS, 1))


def _proj_kernel(h_ref, w_ref, *rest, kind):
    acc = _dot(h_ref[0], w_ref[...])
    if kind == "plain":
        (o_ref,) = rest
        o_ref[0] = acc.astype(o_ref.dtype)
    elif kind == "silu":
        (o_ref,) = rest
        o_ref[0] = (acc * _sigmoid(acc)).astype(o_ref.dtype)
    elif kind == "sigmoid":
        (o_ref,) = rest
        o_ref[0] = _sigmoid(acc).astype(o_ref.dtype)
    elif kind == "rope":
        cos_ref, sin_ref, cscale_ref, o_ref = rest
        cos, sin = cos_ref[...], sin_ref[...]
        for j in range(acc.shape[1] // LANES):
            cols = slice(j * LANES, (j + 1) * LANES)
            a = acc[:, cols]
            o_ref[0, :, cols] = ((a * cos + _rope_partner(a) * sin) * cscale_ref[:, cols]).astype(o_ref.dtype)
    elif kind == "hgate":
        lb_ref, logf_ref, kk_ref = rest
        kk = (1.0 - lb_ref[...]) * _sigmoid(-acc)
        kk_ref[0] = kk
        logf_ref[0] = jnp.log1p(-kk)
    else:
        raise ValueError(kind)


def _proj(h, w, kind, tn, extra=(), out_dtype=BF16):
    B, T, D = h.shape
    n = w.shape[1]
    tm = max(t for t in range(CTX_LEN, PROJ_MAX_TILE + 1, CTX_LEN) if T % t == 0)
    assert n % tn == 0
    grid = (n // tn, B, T // tm)
    in_specs = [pl.BlockSpec((1, tm, D), lambda j, b, i: (b, i, 0)),
                pl.BlockSpec((D, tn), lambda j, b, i: (0, j))]
    out_spec = pl.BlockSpec((1, tm, tn), lambda j, b, i: (b, i, j))
    if kind == "rope":
        in_specs += [pl.BlockSpec((tm, LANES), lambda j, b, i: (i, 0))] * 2
        in_specs += [pl.BlockSpec((1, tn), lambda j, b, i: (0, j))]
        out_specs, out_shape = out_spec, jax.ShapeDtypeStruct((B, T, n), out_dtype)
    elif kind == "hgate":
        in_specs += [pl.BlockSpec((1, tn), lambda j, b, i: (0, j))]
        out_specs = [out_spec, out_spec]
        out_shape = [jax.ShapeDtypeStruct((B, T, n), F32)] * 2
    else:
        out_specs, out_shape = out_spec, jax.ShapeDtypeStruct((B, T, n), out_dtype)
    return pl.pallas_call(
        functools.partial(_proj_kernel, kind=kind),
        grid=grid, in_specs=in_specs, out_specs=out_specs, out_shape=out_shape,
        compiler_params=_cparams(("parallel", "parallel", "parallel")),
        name="proj_" + kind,
    )(h, w, *extra)


def _diff_attn_kernel(lam_ref, subln_ref, q_ref, k_ref, v_ref, o_ref, m_sc, acc_sc, s_sc, *,
                      lam_init, n_lat, tk, latent):
    q = q_ref[0]
    lane = lax.broadcasted_iota(jnp.int32, (1, LANES), 1)
    m_sc[...] = jnp.full(m_sc.shape, MASK_VALUE, F32)
    acc_sc[...] = jnp.zeros(acc_sc.shape, F32)

    def scores(start, size, c):
        k = k_ref[0, pl.ds(start, size), :]
        return _dot_nt(q, jnp.where((lane >= DIFF_DIM) == bool(c), k, jnp.zeros_like(k)))

    def accumulate(start, size, c, slot):
        v = v_ref[0, pl.ds(start, size), :]
        v1 = jnp.concatenate([v, jnp.ones_like(v)], axis=-1)
        for r in range(0, q.shape[0], ATT_ROWS):
            rows = slice(r, r + ATT_ROWS)
            s = s_sc[slot, c, rows, :]
            m_prev = m_sc[c, rows, :]
            m_new = jnp.maximum(m_prev, jnp.max(s, axis=-1, keepdims=True))
            alpha = jnp.exp2(m_prev - m_new)
            p = jnp.exp2((s - m_new[:, :1]).astype(v.dtype))
            acc_sc[c, rows, :] = jnp.concatenate([alpha, alpha], axis=-1) * acc_sc[c, rows, :] + _dot(p, v1)
            m_sc[c, rows, :] = m_new

    starts = list(range(0, k_ref.shape[1], tk)) if latent else [n_lat]
    for c in range(2):
        s_sc[0, c] = scores(starts[0], tk, c)
    for j, start in enumerate(starts):
        slot = j % 2
        for c in range(2):
            if j + 1 < len(starts):
                s_sc[1 - slot, c] = scores(starts[j + 1], tk, c)
            accumulate(start, tk, c, slot)

    lv = lam_ref[...]
    s01 = jnp.sum(lv[0:1] * lv[1:2], axis=-1, keepdims=True)
    s23 = jnp.sum(lv[2:3] * lv[3:4], axis=-1, keepdims=True)
    lam = jnp.exp(s01) - jnp.exp(s23) + lam_init
    a0, a1 = acc_sc[0], acc_sc[1]
    o = a0[:, :LANES] / a0[:, LANES:] - lam * (a1[:, :LANES] / a1[:, LANES:])
    o_ref[0] = (_rms(o) * subln_ref[...] * (1.0 - lam_init)).astype(o_ref.dtype)


def _diff_attention(g1, g2, lam_vecs, subln, lam_init, n_lat, latent):
    B, T, _ = g1.shape
    nh = DIFF_HEADS
    if latent:
        tq, nq, q_off, rows = ATT_TQ, n_lat // ATT_TQ, 0, n_lat
        tk = max(t for t in range(CTX_LEN, ATT_TK + 1, CTX_LEN) if T % t == 0)
    else:
        tq, nq, q_off, rows, tk = CTX_LEN, 1, n_lat // CTX_LEN, CTX_LEN, CTX_LEN
    in_specs = [pl.BlockSpec((4, DIFF_DIM), lambda b, h, i: (0, 0)),
                pl.BlockSpec((1, LANES), lambda b, h, i: (0, 0)),
                pl.BlockSpec((1, tq, LANES), lambda b, h, i: (b, q_off + i, h)),
                pl.BlockSpec((1, T, LANES), lambda b, h, i: (b, 0, nh + h)),
                pl.BlockSpec((1, T, LANES), lambda b, h, i: (b, 0, nh + h))]
    return pl.pallas_call(
        functools.partial(_diff_attn_kernel, lam_init=lam_init, n_lat=n_lat, tk=tk, latent=latent),
        grid=(B, nh, nq), in_specs=in_specs,
        out_specs=pl.BlockSpec((1, tq, LANES), lambda b, h, i: (b, i, h)),
        out_shape=jax.ShapeDtypeStruct((B, rows, nh * LANES), BF16),
        scratch_shapes=[pltpu.VMEM((2, tq, LANES), F32), pltpu.VMEM((2, tq, 2 * LANES), F32),
                        pltpu.VMEM((2, 2, tq, tk), F32)],
        compiler_params=_cparams(("parallel", "parallel", "arbitrary")),
        name="diff_attn_latent" if latent else "diff_attn_ctx",
    )(lam_vecs, subln.reshape(1, LANES), g1, g1, g2)


def _swa_bias(tq):
    half = tq // 2
    assert half == WINDOW
    r = np.arange(tq)[:, None]
    wj = np.arange(2 * tq)[None, :]
    band = np.abs(wj - half - r) <= WINDOW
    variants = [band & (wj >= half), band, band & (wj < tq + half)]
    out = []
    for keep in variants:
        keep = np.concatenate([np.ones((tq, CTX_LEN), bool), keep], axis=1)
        out.append(np.where(keep, 0.0, MASK_VALUE).astype(np.float32))
    return np.stack(out)


def _swa_kernel(sink_ref, q_ref, kc_ref, vc_ref, *rest, latent):
    g = pl.program_id(1)
    if latent:
        bias_ref, kp_ref, kk_ref, kn_ref, vp_ref, vv_ref, vn_ref, o_ref = rest
        k = jnp.concatenate([kc_ref[0], kp_ref[0], kk_ref[0], kn_ref[0]], axis=0)
        v = jnp.concatenate([vc_ref[0], vp_ref[0], vv_ref[0], vn_ref[0]], axis=0)
        bias = bias_ref[0]
    else:
        (o_ref,) = rest
        k, v = kc_ref[0], vc_ref[0]
        bias = None
    lane = lax.broadcasted_iota(jnp.int32, (1, LANES), 1)
    ones = jnp.ones_like(v)
    for pair in range(SWA_GROUP // 2):
        qp = q_ref[0, :, pair * LANES:(pair + 1) * LANES]
        out = None
        for e in range(2):
            half = (lane >= SWA_DIM) == bool(e)
            s = _dot_nt(qp, jnp.where(half, k, jnp.zeros_like(k)))
            if bias is not None:
                s = s + bias
            sink = sink_ref[g * SWA_GROUP + pair * 2 + e] * LOG2E
            m = jnp.maximum(jnp.max(s, axis=-1, keepdims=True), sink)
            p = jnp.exp2((s - m).astype(v.dtype))
            v1 = jnp.concatenate([jnp.where(half, v, jnp.zeros_like(v)), ones], axis=-1)
            pv = _dot(p, v1)
            pv = pv[:, :LANES] / (pv[:, LANES:] + jnp.exp2(sink - m))
            out = pv if out is None else out + pv
        o_ref[0, :, pair * LANES:(pair + 1) * LANES] = out.astype(o_ref.dtype)


def _swa(g1, g2, sink, n_lat, latent):
    B, T, _ = g1.shape
    ctx_blk = n_lat // CTX_LEN
    qcol = 1024 // (2 * LANES)
    kcol = 1536 // LANES
    vcol = 1024 // LANES
    smem = pl.BlockSpec(memory_space=pltpu.SMEM)
    if latent:
        tq = SWA_TQ
        half = tq // 2
        nq = n_lat // tq
        last_half = n_lat // half - 1
        grid = (B, SWA_KV_HEADS, nq)
        ctx_k = pl.BlockSpec((1, CTX_LEN, LANES), lambda b, g, i: (b, ctx_blk, kcol + g))
        ctx_v = pl.BlockSpec((1, CTX_LEN, LANES), lambda b, g, i: (b, ctx_blk, vcol + g))

        def win(col):
            return [pl.BlockSpec((1, half, LANES), lambda b, g, i: (b, jnp.maximum(2 * i - 1, 0), col + g)),
                    pl.BlockSpec((1, tq, LANES), lambda b, g, i: (b, i, col + g)),
                    pl.BlockSpec((1, half, LANES), lambda b, g, i: (b, jnp.minimum(2 * i + 2, last_half), col + g))]

        assert nq >= 2
        bias = _swa_bias(tq)
        bias_spec = pl.BlockSpec((1,) + bias.shape[1:],
                                 lambda b, g, i: (jnp.where(i == 0, 0, jnp.where(i == nq - 1, 2, 1)), 0, 0))
        in_specs = [smem, pl.BlockSpec((1, tq, 2 * LANES), lambda b, g, i: (b, i, qcol + g)),
                    ctx_k, ctx_v, bias_spec] + win(kcol) + win(vcol)
        out_specs = pl.BlockSpec((1, tq, 2 * LANES), lambda b, g, i: (b, i, g))
        out_shape = jax.ShapeDtypeStruct((B, n_lat, 4 * LANES), BF16)
        args = (sink, g1, g1, g2, jnp.asarray(bias), g1, g1, g1, g2, g2, g2)
        sem = ("parallel", "parallel", "parallel")
    else:
        grid = (B, SWA_KV_HEADS)
        in_specs = [smem, pl.BlockSpec((1, CTX_LEN, 2 * LANES), lambda b, g: (b, ctx_blk, qcol + g)),
                    pl.BlockSpec((1, CTX_LEN, LANES), lambda b, g: (b, ctx_blk, kcol + g)),
                    pl.BlockSpec((1, CTX_LEN, LANES), lambda b, g: (b, ctx_blk, vcol + g))]
        out_specs = pl.BlockSpec((1, CTX_LEN, 2 * LANES), lambda b, g: (b, 0, g))
        out_shape = jax.ShapeDtypeStruct((B, CTX_LEN, 4 * LANES), BF16)
        args = (sink, g1, g1, g2)
        sem = ("parallel", "parallel")
    return pl.pallas_call(
        functools.partial(_swa_kernel, latent=latent),
        grid=grid, in_specs=in_specs, out_specs=out_specs, out_shape=out_shape,
        compiler_params=_cparams(sem),
        name="swa_latent" if latent else "swa_ctx",
    )(*args)


def _hgrn_levels():
    h, out = HGRN_SUB, []
    while h < HGRN_CHUNK:
        out.append(h)
        h *= 2
    return out


def _hgrn_constants():
    C, sub = HGRN_CHUNK, HGRN_SUB
    t = np.arange(C)[:, None]
    u = np.arange(C)[None, :]
    masks = [((u // sub) == (t // sub)) & (u <= t)]
    for h in _hgrn_levels():
        masks.append((((t // h) % 2) == 1) & ((u // (2 * h)) == (t // (2 * h))) & (((u // h) % 2) == 0))
    fwd = np.concatenate(masks, axis=0)
    bwd = np.concatenate([m[::-1, ::-1] for m in masks], axis=0)
    mk = np.stack([fwd, bwd]).astype(np.float32)
    rows = np.arange(sub * HGRN_DK)[:, None] // HGRN_DK
    esel = (rows == (np.arange(C)[None, :] % sub)).astype(np.float32)
    return mk, esel


def _chunk_cumsum(x, reverse):
    n = x.shape[0]
    row = lax.broadcasted_iota(jnp.int32, x.shape, 0)
    d = 1
    while d < n:
        if reverse:
            x = x + jnp.where(row < n - d, pltpu.roll(x, n - d, 0), 0.0)
        else:
            x = x + jnp.where(row >= d, pltpu.roll(x, d, 0), 0.0)
        d *= 2
    return x


def _hgrn_chunk(q, kk, lf, v, st, masks, esel, reverse):
    C, sub, dk = HGRN_CHUNK, HGRN_SUB, HGRN_DK
    b = _chunk_cumsum(lf * LOG2E, reverse)
    q3, k3, b3 = (a.reshape(C // sub, sub, dk) for a in (q, kk, b))
    tiles = []
    for sg in range(sub):
        d = jnp.minimum(b3 - b3[:, sg:sg + 1, :], 0.0)
        tiles.append((q3 * k3[:, sg:sg + 1, :] * jnp.exp2(d)).reshape(C, dk).astype(BF16))
    a_mat = jnp.where(masks[0], _dot(jnp.concatenate(tiles, axis=-1), esel), 0.0)
    for n, h in enumerate(_hgrn_levels()):
        bh = b.reshape(C // (2 * h), 2 * h, dk)
        r = lax.broadcasted_iota(jnp.int32, bh.shape, 1)
        if reverse:
            anchor, is_query = bh[:, h:h + 1, :], r < h
        else:
            anchor, is_query = bh[:, h - 1:h, :], r >= h
        w = jnp.exp2(jnp.where(is_query, bh - anchor, anchor - bh)).reshape(C, dk)
        a_mat = a_mat + jnp.where(masks[n + 1], _dot_nt((q * w).astype(BF16), (kk * w).astype(BF16)), 0.0)
    o = _dot_nt((q * jnp.exp2(b)).astype(BF16), st.astype(BF16)) + _dot(a_mat.astype(BF16), v)
    total = b[0:1] if reverse else b[C - 1:C]
    st_new = st * jnp.exp2(total) + _dot_tn(v, (kk * jnp.exp2(total - b)).astype(BF16))
    return o, st_new


def _hgrn_kernel(mk_ref, esel_ref, qf_ref, lff_ref, kkf_ref, vf_ref, qb_ref, lfb_ref, kkb_ref, vb_ref,
                 of_ref, ob_ref, st_sc):
    C = HGRN_CHUNK
    nmask = len(_hgrn_levels()) + 1

    @pl.when(pl.program_id(1) == 0)
    def _():
        st_sc[...] = jnp.zeros(st_sc.shape, F32)

    esel = esel_ref[...]
    dirs = ((qf_ref, lff_ref, kkf_ref, vf_ref, of_ref), (qb_ref, lfb_ref, kkb_ref, vb_ref, ob_ref))
    n_sub = qf_ref.shape[1] // C
    for d, (q_ref, lf_ref, kk_ref, v_ref, o_ref) in enumerate(dirs):
        masks = [mk_ref[d, n * C:(n + 1) * C, :] > 0.5 for n in range(nmask)]
        for hh in range(HGRN_HEADS):
            sl = slice(hh * HGRN_DK, (hh + 1) * HGRN_DK)
            st = st_sc[d, hh]
            for ci in (range(n_sub - 1, -1, -1) if d else range(n_sub)):
                rows = slice(ci * C, (ci + 1) * C)
                o, st = _hgrn_chunk(q_ref[0, rows, sl].astype(F32), kk_ref[0, rows, sl], lf_ref[0, rows, sl],
                                    v_ref[0, rows, sl], st, masks, esel, reverse=bool(d))
                o_ref[0, rows, sl] = o
            st_sc[d, hh] = st


def _hgrn(g3, logf, kk, g2, n_lat):
    B, T, _ = g3.shape
    C = HGRN_CHUNK * HGRN_STEP_CHUNKS
    width = HGRN_HEADS * HGRN_DK
    nl, nc = n_lat // C, CTX_LEN // C
    assert n_lat % C == 0 and CTX_LEN % C == 0
    mk, esel = _hgrn_constants()
    row_f = lambda i: jnp.where(i < nc, nl + i, i - nc)
    row_b = lambda i: nl + nc - 1 - i

    def tok(row, col):
        return pl.BlockSpec((1, C, width), lambda b, i: (b, row(i), col))

    const = lambda a: pl.BlockSpec(a.shape, lambda b, i: (0,) * a.ndim)
    out = jax.ShapeDtypeStruct((B, T, width), F32)
    return pl.pallas_call(
        _hgrn_kernel,
        grid=(B, T // C),
        in_specs=[const(mk), const(esel),
                  tok(row_f, 0), tok(row_f, 0), tok(row_f, 0), tok(row_f, 0),
                  tok(row_b, 0), tok(row_b, 1), tok(row_b, 1), tok(row_b, 0)],
        out_specs=[tok(row_f, 0), tok(row_b, 0)],
        out_shape=[out, out],
        scratch_shapes=[pltpu.VMEM((2, HGRN_HEADS, HGRN_DK, HGRN_DK), F32)],
        compiler_params=_cparams(("parallel", "arbitrary")),
        name="hgrn",
    )(jnp.asarray(mk, F32), jnp.asarray(esel, BF16), g3, logf, kk, g2, g3, logf, kk, g2)


def _merge_kernel(x_ref, ya_ref, of_ref, ob_ref, sg_ref, yc_ref, gt_ref, mod_ref, ng_ref, hn_ref,
                  wb_ref, wo_ref, xo_ref, h_ref):
    o = of_ref[0] + ob_ref[0]
    sg = sg_ref[0].astype(F32)
    yb = jnp.concatenate(
        [_rms(o[:, h * LANES:(h + 1) * LANES]) * hn_ref[...] * sg[:, h * LANES:(h + 1) * LANES]
         for h in range(HGRN_HEADS)], axis=-1).astype(BF16)
    ys = (ya_ref[0], yb, yc_ref[0])
    merged = None
    for j in range(3):
        gate = gt_ref[0, :, j * D_MODEL:(j + 1) * D_MODEL].astype(F32)
        term = gate * _dot(ys[j], wb_ref[j])
        merged = term if merged is None else merged + term
    y = _dot(merged.astype(BF16), wo_ref[...])
    x_new = x_ref[0] + _mod_slices(mod_ref, 2) * (_rms(y) * ng_ref[1:2])
    xo_ref[0] = x_new
    h2 = _rms(x_new) * ng_ref[2:3]
    h_ref[0] = (h2 * (1.0 + _mod_slices(mod_ref, 4)) + _mod_slices(mod_ref, 3)).astype(h_ref.dtype)


def _merge(xc, ya, o_f, o_b, g3, yc, g5, modsel, norm_g, hgrn_norm, wb, wo, rows, n_lat_tiles):
    B, _, D = xc.shape
    tm = TOKEN_TILE
    tokw = lambda w, col=0: pl.BlockSpec((1, tm, w), lambda b, i: (b, i, col))
    full = lambda a: pl.BlockSpec(a.shape, lambda b, i: (0,) * a.ndim, pipeline_mode=pl.Buffered(1))
    hn = hgrn_norm.reshape(1, LANES)
    return pl.pallas_call(
        _merge_kernel,
        grid=(B, pl.cdiv(rows, tm)),
        in_specs=[tokw(D), tokw(512), tokw(512), tokw(512), tokw(512, 1), tokw(512), tokw(3 * D),
                  _mod_spec(n_lat_tiles), full(norm_g), full(hn), full(wb), full(wo)],
        out_specs=[tokw(D), tokw(D)],
        out_shape=[jax.ShapeDtypeStruct((B, rows, D), F32), jax.ShapeDtypeStruct((B, rows, D), BF16)],
        compiler_params=_cparams(("parallel", "parallel")),
        name="merge",
    )(xc, ya, o_f, o_b, g3, yc, g5, modsel, norm_g, hn, wb, wo)


def _mlp_kernel(x_ref, h_ref, mod_ref, ng_ref, wu_ref, wd_ref, o_ref):
    u = jnp.maximum(_dot(h_ref[0], wu_ref[...]), 0.0)
    y = _dot((u * u).astype(BF16), wd_ref[...])
    o_ref[0] = x_ref[0] + _mod_slices(mod_ref, 5) * (_rms(y) * ng_ref[3:4])


def _mlp(x_new, h2, modsel, norm_g, wu, wd, n_lat_tiles):
    B, rows, D = x_new.shape
    tm = TOKEN_TILE
    tok = pl.BlockSpec((1, tm, D), lambda b, i: (b, i, 0))
    full = lambda a: pl.BlockSpec(a.shape, lambda b, i: (0,) * a.ndim, pipeline_mode=pl.Buffered(1))
    return pl.pallas_call(
        _mlp_kernel,
        grid=(B, pl.cdiv(rows, tm)),
        in_specs=[tok, tok, _mod_spec(n_lat_tiles), full(norm_g), full(wu), full(wd)],
        out_specs=tok,
        out_shape=jax.ShapeDtypeStruct((B, rows, D), F32),
        compiler_params=_cparams(("parallel", "parallel")),
        name="mlp",
    )(x_new, h2, modsel, norm_g, wu, wd)


def _rope_tables(n_lat):
    t = jnp.arange(n_lat)
    pos = jnp.stack([t // GRID_W, t % GRID_W], axis=-1).astype(F32)
    inv_freq = ROPE_BASE ** (-jnp.arange(ROPE_FREQS, dtype=F32) / ROPE_FREQS)
    ang = pos[:, :, None] * inv_freq
    cos, sin = jnp.cos(ang), jnp.sin(ang)
    cos64 = jnp.concatenate([cos, cos], axis=-1).reshape(n_lat, 4 * ROPE_FREQS)
    sin64 = jnp.concatenate([-sin, sin], axis=-1).reshape(n_lat, 4 * ROPE_FREQS)
    cos_t = jnp.concatenate([jnp.tile(cos64, (1, 2)), jnp.ones((CTX_LEN, LANES), F32)], axis=0)
    sin_t = jnp.concatenate([jnp.tile(sin64, (1, 2)), jnp.zeros((CTX_LEN, LANES), F32)], axis=0)
    return cos_t, sin_t


def _cols(w, *names):
    return [w[:, _OFF[n][0]:_OFF[n][1]] for n in names]


def _dup_heads(w):
    a, b = w[:, :SWA_DIM], w[:, SWA_DIM:]
    return jnp.concatenate([a, a, b, b], axis=1)


def kernel(x, c, ctx, c_ctx, w_ada, b_ada, norm_g, w_in, diff_lambda, diff_subln, hgrn_lb, hgrn_norm,
           swa_sink, w_branch, w_out, w_mlp_up, w_mlp_down):
    B, S, D = x.shape
    depth = w_ada.shape[0]
    assert D == D_MODEL and ctx.shape[1] == CTX_LEN and S % ATT_TQ == 0
    n_lat_tiles = S // TOKEN_TILE

    cos_t, sin_t = _rope_tables(S)
    qscale = jnp.concatenate([jnp.full((1, 512), DIFF_DIM ** -0.5 * LOG2E, F32), jnp.ones((1, 512), F32),
                              jnp.full((1, 512), SWA_DIM ** -0.5 * LOG2E, F32), jnp.ones((1, 256), F32)], axis=1)
    cvec = jnp.concatenate([c, c_ctx[None], jnp.zeros((8 - B - 1, D), F32)], axis=0)
    mods = _ada(cvec, w_ada, b_ada)
    lower = _lower_bounds(hgrn_lb)
    xc = jnp.concatenate([x, ctx], axis=1)

    for l in range(depth):
        need_ctx = l < depth - 1
        lam_init = 0.8 - 0.6 * math.exp(-0.3 * l)
        modsel = jnp.stack([mods[l, :B], jnp.broadcast_to(mods[l, B], (B, 6 * D))], axis=1)[:, :, None, :]
        wl = w_in[l]
        dq, dk, dv, hq, hff, hfb, hi, hg, sq, sk, sv, gt = _cols(
            wl, "dq", "dk", "dv", "hq", "hff", "hfb", "hi", "hg", "sq", "sk", "sv", "gt")
        w1 = jnp.concatenate([dq, dk, sq, _dup_heads(sk)], axis=1).astype(BF16)
        w2 = jnp.concatenate([hi, dv, _dup_heads(sv)], axis=1).astype(BF16)
        w3 = jnp.concatenate([hq, hg], axis=1).astype(BF16)
        w4 = jnp.concatenate([hff, hfb], axis=1).astype(BF16)
        w5 = gt.astype(BF16)

        h = _norm_mod(xc, norm_g[l, 0], modsel, n_lat_tiles)
        g1 = _proj(h, w1, "rope", 896, extra=(cos_t, sin_t, qscale))
        g2 = _proj(h, w2, "plain", 1280)
        g3 = _proj(h, w3, "silu", 1024)
        logf, kk = _proj(h, w4, "hgate", 1024, extra=(lower[l].reshape(1, 1024),))
        g5 = _proj(h, w5, "sigmoid", 1024)

        ya = _diff_attention(g1, g2, diff_lambda[l], diff_subln[l], lam_init, S, latent=True)
        yc = _swa(g1, g2, swa_sink[l], S, latent=True)
        if need_ctx:
            ya = jnp.concatenate(
                [ya, _diff_attention(g1, g2, diff_lambda[l], diff_subln[l], lam_init, S, latent=False)], axis=1)
            yc = jnp.concatenate([yc, _swa(g1, g2, swa_sink[l], S, latent=False)], axis=1)
        o_f, o_b = _hgrn(g3, logf, kk, g2, S)

        rows = S + CTX_LEN if need_ctx else S
        x_new, h2 = _merge(xc, ya, o_f, o_b, g3, yc, g5, modsel, norm_g[l], hgrn_norm[l],
                           w_branch[l].astype(BF16), w_out[l].astype(BF16), rows, n_lat_tiles)
        xc = _mlp(x_new, h2, modsel, norm_g[l], w_mlp_up[l].astype(BF16), w_mlp_down[l].astype(BF16),
                  n_lat_tiles)
    return xc[:, :S]
```

```python
import functools
import math

import numpy as np
import jax
import jax.numpy as jnp
from jax import lax
from jax.experimental import pallas as pl
from jax.experimental.pallas import tpu as pltpu

F32 = jnp.float32
BF16 = jnp.bfloat16

D_MODEL = 1024
GRID_W = 64
CTX_LEN = 256
ROPE_BASE = 10000.0
ROPE_FREQS = 16
NORM_EPS = 1e-6
MASK_VALUE = -1e30
LOG2E = math.log2(math.e)

DIFF_HEADS = 4
DIFF_DIM = 64
HGRN_HEADS = 4
HGRN_DK = 128
SWA_KV_HEADS = 2
SWA_GROUP = 4
SWA_DIM = 64
WINDOW = 128
D_FF = 4 * D_MODEL

LANES = 128
TOKEN_TILE = 512
PROJ_MAX_TILE = 1024
HGRN_CHUNK = 64
HGRN_SUB = 8
HGRN_STEP_CHUNKS = 4
ATT_TQ = 1024
ATT_TK = 768
ATT_ROWS = 512
SWA_TQ = 256
VMEM_LIMIT = 56 * 1024 * 1024

_OFF = {}
_o = 0
for _n, _w in (("dq", 512), ("dk", 512), ("dv", 512), ("hq", 512), ("hff", 512), ("hfb", 512),
               ("hi", 512), ("hg", 512), ("sq", 512), ("sk", 128), ("sv", 128), ("gt", 3072)):
    _OFF[_n] = (_o, _o + _w)
    _o += _w


def _sigmoid(x):
    return 1.0 / (1.0 + jnp.exp(-x))


def _dot(a, b):
    return jnp.dot(a, b, preferred_element_type=F32)


def _dot_nt(a, b):
    return lax.dot_general(a, b, (((1,), (1,)), ((), ())), preferred_element_type=F32)


def _dot_tn(a, b):
    return lax.dot_general(a, b, (((0,), (0,)), ((), ())), preferred_element_type=F32)


def _rms(x, eps=NORM_EPS):
    return x * lax.rsqrt(jnp.mean(x * x, axis=-1, keepdims=True) + eps)


def _cparams(sem):
    return pltpu.CompilerParams(dimension_semantics=sem, vmem_limit_bytes=VMEM_LIMIT)


def _ada_kernel(c_ref, w_ref, b_ref, o_ref):
    c = c_ref[...]
    a = c * _sigmoid(c)
    o_ref[0] = _dot(a, w_ref[0]) + b_ref[0]


def _ada(cvec, w_ada, b_ada):
    depth, d, n = w_ada.shape
    tn = 1024
    return pl.pallas_call(
        _ada_kernel,
        grid=(depth, n // tn),
        in_specs=[pl.BlockSpec((8, d), lambda l, j: (0, 0)),
                  pl.BlockSpec((1, d, tn), lambda l, j: (l, 0, j)),
                  pl.BlockSpec((1, 1, tn), lambda l, j: (l, 0, j))],
        out_specs=pl.BlockSpec((1, 8, tn), lambda l, j: (l, 0, j)),
        out_shape=jax.ShapeDtypeStruct((depth, 8, n), F32),
        compiler_params=_cparams(("arbitrary", "arbitrary")),
        name="ada",
    )(cvec, w_ada, b_ada.reshape(depth, 1, n))


def _lb_kernel(p_ref, o_ref):
    p = p_ref[...]
    e = jnp.exp(p - jnp.max(p, axis=0, keepdims=True))
    sm = e / jnp.sum(e, axis=0, keepdims=True)
    run = jnp.zeros_like(sm[0])
    for l in range(p.shape[0]):
        run = run + sm[l]
        o_ref[l] = run - sm[0]


def _lower_bounds(hgrn_lb):
    return pl.pallas_call(
        _lb_kernel,
        out_shape=jax.ShapeDtypeStruct(hgrn_lb.shape, F32),
        name="hgrn_lower_bounds",
    )(hgrn_lb.astype(F32))


def _mod_slices(mod_ref, idx):
    return mod_ref[0, 0, :, idx * D_MODEL:(idx + 1) * D_MODEL]


def _norm_mod_kernel(x_ref, g_ref, mod_ref, o_ref):
    y = _rms(x_ref[0]) * g_ref[...]
    o_ref[0] = (y * (1.0 + _mod_slices(mod_ref, 1)) + _mod_slices(mod_ref, 0)).astype(o_ref.dtype)


def _mod_spec(n_lat_tiles):
    return pl.BlockSpec((1, 1, 1, 6 * D_MODEL), lambda b, i: (b, i // n_lat_tiles, 0, 0))


def _norm_mod(xc, g, modsel, n_lat_tiles):
    B, T, D = xc.shape
    return pl.pallas_call(
        _norm_mod_kernel,
        grid=(B, pl.cdiv(T, TOKEN_TILE)),
        in_specs=[pl.BlockSpec((1, TOKEN_TILE, D), lambda b, i: (b, i, 0)),
                  pl.BlockSpec((1, D), lambda b, i: (0, 0)),
                  _mod_spec(n_lat_tiles)],
        out_specs=pl.BlockSpec((1, TOKEN_TILE, D), lambda b, i: (b, i, 0)),
        out_shape=jax.ShapeDtypeStruct((B, T, D), BF16),
        compiler_params=_cparams(("parallel", "parallel")),
        name="norm_mod",
    )(xc, g.reshape(1, D), modsel)


def _rope_partner(x):
    lane = lax.broadcasted_iota(jnp.int32, x.shape, 1)
    first = (lane % 32) < ROPE_FREQS
    return jnp.where(first, pltpu.roll(x, LANES - ROPE_FREQS, 1), pltpu.roll(x, ROPE_FREQS, 1))


def _proj_kernel(h_ref, w_ref, *rest, kind):
    acc = _dot(h_ref[0], w_ref[...])
    if kind == "plain":
        (o_ref,) = rest
        o_ref[0] = acc.astype(o_ref.dtype)
    elif kind == "silu":
        (o_ref,) = rest
        o_ref[0] = (acc * _sigmoid(acc)).astype(o_ref.dtype)
    elif kind == "sigmoid":
        (o_ref,) = rest
        o_ref[0] = _sigmoid(acc).astype(o_ref.dtype)
    elif kind == "rope":
        cos_ref, sin_ref, cscale_ref, o_ref = rest
        cos, sin = cos_ref[...], sin_ref[...]
        for j in range(acc.shape[1] // LANES):
            cols = slice(j * LANES, (j + 1) * LANES)
            a = acc[:, cols]
            o_ref[0, :, cols] = ((a * cos + _rope_partner(a) * sin) * cscale_ref[:, cols]).astype(o_ref.dtype)
    elif kind == "hgate":
        lb_ref, logf_ref, kk_ref = rest
        kk = (1.0 - lb_ref[...]) * _sigmoid(-acc)
        kk_ref[0] = kk
        logf_ref[0] = jnp.log(1.0 - kk)
    else:
        raise ValueError(kind)


def _proj(h, w, kind, tn, extra=(), out_dtype=BF16):
    B, T, D = h.shape
    n = w.shape[1]
    tm = max(t for t in range(CTX_LEN, PROJ_MAX_TILE + 1, CTX_LEN) if T % t == 0)
    assert n % tn == 0
    grid = (n // tn, B, T // tm)
    in_specs = [pl.BlockSpec((1, tm, D), lambda j, b, i: (b, i, 0)),
                pl.BlockSpec((D, tn), lambda j, b, i: (0, j))]
    out_spec = pl.BlockSpec((1, tm, tn), lambda j, b, i: (b, i, j))
    if kind == "rope":
        in_specs += [pl.BlockSpec((tm, LANES), lambda j, b, i: (i, 0))] * 2
        in_specs += [pl.BlockSpec((1, tn), lambda j, b, i: (0, j))]
        out_specs, out_shape = out_spec, jax.ShapeDtypeStruct((B, T, n), out_dtype)
    elif kind == "hgate":
        in_specs += [pl.BlockSpec((1, tn), lambda j, b, i: (0, j))]
        out_specs = [out_spec, out_spec]
        out_shape = [jax.ShapeDtypeStruct((B, T, n), F32)] * 2
    else:
        out_specs, out_shape = out_spec, jax.ShapeDtypeStruct((B, T, n), out_dtype)
    return pl.pallas_call(
        functools.partial(_proj_kernel, kind=kind),
        grid=grid, in_specs=in_specs, out_specs=out_specs, out_shape=out_shape,
        compiler_params=_cparams(("parallel", "parallel", "parallel")),
        name="proj_" + kind,
    )(h, w, *extra)


def _diff_attn_kernel(lam_ref, subln_ref, q_ref, k_ref, v_ref, o_ref, m_sc, acc_sc, s_sc, *,
                      lam_init, n_lat, tk, latent):
    q = q_ref[0]
    lane = lax.broadcasted_iota(jnp.int32, (1, LANES), 1)
    m_sc[...] = jnp.full(m_sc.shape, MASK_VALUE, F32)
    acc_sc[...] = jnp.zeros(acc_sc.shape, F32)

    def scores(start, size, c):
        k = k_ref[0, pl.ds(start, size), :]
        return _dot_nt(q, jnp.where((lane >= DIFF_DIM) == bool(c), k, jnp.zeros_like(k)))

    def accumulate(start, size, c, slot):
        v = v_ref[0, pl.ds(start, size), :]
        v1 = jnp.concatenate([v, jnp.ones_like(v)], axis=-1)
        for r in range(0, q.shape[0], ATT_ROWS):
            rows = slice(r, r + ATT_ROWS)
            s = s_sc[slot, c, rows, :]
            m_prev = m_sc[c, rows, :]
            m_new = jnp.maximum(m_prev, jnp.max(s, axis=-1, keepdims=True))
            alpha = jnp.exp2(m_prev - m_new)
            p = jnp.exp2((s - m_new[:, :1]).astype(v.dtype))
            acc_sc[c, rows, :] = jnp.concatenate([alpha, alpha], axis=-1) * acc_sc[c, rows, :] + _dot(p, v1)
            m_sc[c, rows, :] = m_new

    starts = list(range(0, k_ref.shape[1], tk)) if latent else [n_lat]
    for c in range(2):
        s_sc[0, c] = scores(starts[0], tk, c)
    for j, start in enumerate(starts):
        slot = j % 2
        for c in range(2):
            if j + 1 < len(starts):
                s_sc[1 - slot, c] = scores(starts[j + 1], tk, c)
            accumulate(start, tk, c, slot)

    lv = lam_ref[...]
    s01 = jnp.sum(lv[0:1] * lv[1:2], axis=-1, keepdims=True)
    s23 = jnp.sum(lv[2:3] * lv[3:4], axis=-1, keepdims=True)
    lam = jnp.exp(s01) - jnp.exp(s23) + lam_init
    a0, a1 = acc_sc[0], acc_sc[1]
    o = a0[:, :LANES] / a0[:, LANES:] - lam * (a1[:, :LANES] / a1[:, LANES:])
    o_ref[0] = (_rms(o) * subln_ref[...] * (1.0 - lam_init)).astype(o_ref.dtype)


def _diff_attention(g1, g2, lam_vecs, subln, lam_init, n_lat, latent):
    B, T, _ = g1.shape
    nh = DIFF_HEADS
    if latent:
        tq, nq, q_off, rows = ATT_TQ, n_lat // ATT_TQ, 0, n_lat
        tk = max(t for t in range(CTX_LEN, ATT_TK + 1, CTX_LEN) if T % t == 0)
    else:
        tq, nq, q_off, rows, tk = CTX_LEN, 1, n_lat // CTX_LEN, CTX_LEN, CTX_LEN
    in_specs = [pl.BlockSpec((4, DIFF_DIM), lambda b, h, i: (0, 0)),
                pl.BlockSpec((1, LANES), lambda b, h, i: (0, 0)),
                pl.BlockSpec((1, tq, LANES), lambda b, h, i: (b, q_off + i, h)),
                pl.BlockSpec((1, T, LANES), lambda b, h, i: (b, 0, nh + h)),
                pl.BlockSpec((1, T, LANES), lambda b, h, i: (b, 0, nh + h))]
    return pl.pallas_call(
        functools.partial(_diff_attn_kernel, lam_init=lam_init, n_lat=n_lat, tk=tk, latent=latent),
        grid=(B, nh, nq), in_specs=in_specs,
        out_specs=pl.BlockSpec((1, tq, LANES), lambda b, h, i: (b, i, h)),
        out_shape=jax.ShapeDtypeStruct((B, rows, nh * LANES), BF16),
        scratch_shapes=[pltpu.VMEM((2, tq, LANES), F32), pltpu.VMEM((2, tq, 2 * LANES), F32),
                        pltpu.VMEM((2, 2, tq, tk), F32)],
        compiler_params=_cparams(("parallel", "parallel", "arbitrary")),
        name="diff_attn_latent" if latent else "diff_attn_ctx",
    )(lam_vecs, subln.reshape(1, LANES), g1, g1, g2)


def _swa_bias(tq):
    half = tq // 2
    assert half == WINDOW
    r = np.arange(tq)[:, None]
    wj = np.arange(2 * tq)[None, :]
    band = np.abs(wj - half - r) <= WINDOW
    variants = [band & (wj >= half), band, band & (wj < tq + half)]
    out = []
    for keep in variants:
        keep = np.concatenate([np.ones((tq, CTX_LEN), bool), keep], axis=1)
        out.append(np.where(keep, 0.0, MASK_VALUE).astype(np.float32))
    return np.stack(out)


def _swa_kernel(sink_ref, q_ref, kc_ref, vc_ref, *rest, latent):
    g = pl.program_id(1)
    if latent:
        bias_ref, kp_ref, kk_ref, kn_ref, vp_ref, vv_ref, vn_ref, o_ref = rest
        k = jnp.concatenate([kc_ref[0], kp_ref[0], kk_ref[0], kn_ref[0]], axis=0)
        v = jnp.concatenate([vc_ref[0], vp_ref[0], vv_ref[0], vn_ref[0]], axis=0)
        bias = bias_ref[0]
    else:
        (o_ref,) = rest
        k, v = kc_ref[0], vc_ref[0]
        bias = None
    lane = lax.broadcasted_iota(jnp.int32, (1, LANES), 1)
    ones = jnp.ones_like(v)
    for pair in range(SWA_GROUP // 2):
        qp = q_ref[0, :, pair * LANES:(pair + 1) * LANES]
        out = None
        for e in range(2):
            half = (lane >= SWA_DIM) == bool(e)
            s = _dot_nt(qp, jnp.where(half, k, jnp.zeros_like(k)))
            if bias is not None:
                s = s + bias
            sink = sink_ref[g * SWA_GROUP + pair * 2 + e] * LOG2E
            m = jnp.maximum(jnp.max(s, axis=-1, keepdims=True), sink)
            p = jnp.exp2((s - m).astype(v.dtype))
            v1 = jnp.concatenate([jnp.where(half, v, jnp.zeros_like(v)), ones], axis=-1)
            pv = _dot(p, v1)
            pv = pv[:, :LANES] / (pv[:, LANES:] + jnp.exp2(sink - m))
            out = pv if out is None else out + pv
        o_ref[0, :, pair * LANES:(pair + 1) * LANES] = out.astype(o_ref.dtype)


def _swa(g1, g2, sink, n_lat, latent):
    B, T, _ = g1.shape
    ctx_blk = n_lat // CTX_LEN
    qcol = 1024 // (2 * LANES)
    kcol = 1536 // LANES
    vcol = 1024 // LANES
    smem = pl.BlockSpec(memory_space=pltpu.SMEM)
    if latent:
        tq = SWA_TQ
        half = tq // 2
        nq = n_lat // tq
        last_half = n_lat // half - 1
        grid = (B, SWA_KV_HEADS, nq)
        ctx_k = pl.BlockSpec((1, CTX_LEN, LANES), lambda b, g, i: (b, ctx_blk, kcol + g))
        ctx_v = pl.BlockSpec((1, CTX_LEN, LANES), lambda b, g, i: (b, ctx_blk, vcol + g))

        def win(col):
            return [pl.BlockSpec((1, half, LANES), lambda b, g, i: (b, jnp.maximum(2 * i - 1, 0), col + g)),
                    pl.BlockSpec((1, tq, LANES), lambda b, g, i: (b, i, col + g)),
                    pl.BlockSpec((1, half, LANES), lambda b, g, i: (b, jnp.minimum(2 * i + 2, last_half), col + g))]

        assert nq >= 2
        bias = _swa_bias(tq)
        bias_spec = pl.BlockSpec((1,) + bias.shape[1:],
                                 lambda b, g, i: (jnp.where(i == 0, 0, jnp.where(i == nq - 1, 2, 1)), 0, 0))
        in_specs = [smem, pl.BlockSpec((1, tq, 2 * LANES), lambda b, g, i: (b, i, qcol + g)),
                    ctx_k, ctx_v, bias_spec] + win(kcol) + win(vcol)
        out_specs = pl.BlockSpec((1, tq, 2 * LANES), lambda b, g, i: (b, i, g))
        out_shape = jax.ShapeDtypeStruct((B, n_lat, 4 * LANES), BF16)
        args = (sink, g1, g1, g2, jnp.asarray(bias), g1, g1, g1, g2, g2, g2)
        sem = ("parallel", "parallel", "parallel")
    else:
        grid = (B, SWA_KV_HEADS)
        in_specs = [smem, pl.BlockSpec((1, CTX_LEN, 2 * LANES), lambda b, g: (b, ctx_blk, qcol + g)),
                    pl.BlockSpec((1, CTX_LEN, LANES), lambda b, g: (b, ctx_blk, kcol + g)),
                    pl.BlockSpec((1, CTX_LEN, LANES), lambda b, g: (b, ctx_blk, vcol + g))]
        out_specs = pl.BlockSpec((1, CTX_LEN, 2 * LANES), lambda b, g: (b, 0, g))
        out_shape = jax.ShapeDtypeStruct((B, CTX_LEN, 4 * LANES), BF16)
        args = (sink, g1, g1, g2)
        sem = ("parallel", "parallel")
    return pl.pallas_call(
        functools.partial(_swa_kernel, latent=latent),
        grid=grid, in_specs=in_specs, out_specs=out_specs, out_shape=out_shape,
        compiler_params=_cparams(sem),
        name="swa_latent" if latent else "swa_ctx",
    )(*args)


def _hgrn_levels():
    h, out = HGRN_SUB, []
    while h < HGRN_CHUNK:
        out.append(h)
        h *= 2
    return out


def _hgrn_constants():
    C, sub = HGRN_CHUNK, HGRN_SUB
    t = np.arange(C)[:, None]
    u = np.arange(C)[None, :]
    masks = [((u // sub) == (t // sub)) & (u <= t)]
    for h in _hgrn_levels():
        masks.append((((t // h) % 2) == 1) & ((u // (2 * h)) == (t // (2 * h))) & (((u // h) % 2) == 0))
    fwd = np.concatenate(masks, axis=0)
    bwd = np.concatenate([m[::-1, ::-1] for m in masks], axis=0)
    mk = np.stack([fwd, bwd]).astype(np.float32)
    rows = np.arange(sub * HGRN_DK)[:, None] // HGRN_DK
    esel = (rows == (np.arange(C)[None, :] % sub)).astype(np.float32)
    return mk, esel


def _chunk_cumsum(x, reverse):
    n = x.shape[0]
    row = lax.broadcasted_iota(jnp.int32, x.shape, 0)
    d = 1
    while d < n:
        if reverse:
            x = x + jnp.where(row < n - d, pltpu.roll(x, n - d, 0), 0.0)
        else:
            x = x + jnp.where(row >= d, pltpu.roll(x, d, 0), 0.0)
        d *= 2
    return x


def _hgrn_chunk(q, kk, lf, v, st, masks, esel, reverse):
    C, sub, dk = HGRN_CHUNK, HGRN_SUB, HGRN_DK
    b = _chunk_cumsum(lf * LOG2E, reverse)
    q3, k3, b3 = (a.reshape(C // sub, sub, dk) for a in (q, kk, b))
    tiles = []
    for sg in range(sub):
        d = jnp.minimum(b3 - b3[:, sg:sg + 1, :], 0.0)
        tiles.append((q3 * k3[:, sg:sg + 1, :] * jnp.exp2(d)).reshape(C, dk).astype(BF16))
    a_mat = jnp.where(masks[0], _dot(jnp.concatenate(tiles, axis=-1), esel), 0.0)
    for n, h in enumerate(_hgrn_levels()):
        bh = b.reshape(C // (2 * h), 2 * h, dk)
        r = lax.broadcasted_iota(jnp.int32, bh.shape, 1)
        if reverse:
            anchor, is_query = bh[:, h:h + 1, :], r < h
        else:
            anchor, is_query = bh[:, h - 1:h, :], r >= h
        w = jnp.exp2(jnp.where(is_query, bh - anchor, anchor - bh)).reshape(C, dk)
        a_mat = a_mat + jnp.where(masks[n + 1], _dot_nt((q * w).astype(BF16), (kk * w).astype(BF16)), 0.0)
    o = _dot_nt((q * jnp.exp2(b)).astype(BF16), st.astype(BF16)) + _dot(a_mat.astype(BF16), v)
    total = b[0:1] if reverse else b[C - 1:C]
    st_new = st * jnp.exp2(total) + _dot_tn(v, (kk * jnp.exp2(total - b)).astype(BF16))
    return o, st_new


def _hgrn_kernel(mk_ref, esel_ref, qf_ref, lff_ref, kkf_ref, vf_ref, qb_ref, lfb_ref, kkb_ref, vb_ref,
                 of_ref, ob_ref, st_sc):
    C = HGRN_CHUNK
    nmask = len(_hgrn_levels()) + 1

    @pl.when(pl.program_id(1) == 0)
    def _():
        st_sc[...] = jnp.zeros(st_sc.shape, F32)

    esel = esel_ref[...]
    dirs = ((qf_ref, lff_ref, kkf_ref, vf_ref, of_ref), (qb_ref, lfb_ref, kkb_ref, vb_ref, ob_ref))
    n_sub = qf_ref.shape[1] // C
    for d, (q_ref, lf_ref, kk_ref, v_ref, o_ref) in enumerate(dirs):
        masks = [mk_ref[d, n * C:(n + 1) * C, :] > 0.5 for n in range(nmask)]
        for hh in range(HGRN_HEADS):
            sl = slice(hh * HGRN_DK, (hh + 1) * HGRN_DK)
            st = st_sc[d, hh]
            for ci in (range(n_sub - 1, -1, -1) if d else range(n_sub)):
                rows = slice(ci * C, (ci + 1) * C)
                o, st = _hgrn_chunk(q_ref[0, rows, sl].astype(F32), kk_ref[0, rows, sl], lf_ref[0, rows, sl],
                                    v_ref[0, rows, sl], st, masks, esel, reverse=bool(d))
                o_ref[0, rows, sl] = o
            st_sc[d, hh] = st


def _hgrn(g3, logf, kk, g2, n_lat):
    B, T, _ = g3.shape
    C = HGRN_CHUNK * HGRN_STEP_CHUNKS
    width = HGRN_HEADS * HGRN_DK
    nl, nc = n_lat // C, CTX_LEN // C
    assert n_lat % C == 0 and CTX_LEN % C == 0
    mk, esel = _hgrn_constants()
    row_f = lambda i: jnp.where(i < nc, nl + i, i - nc)
    row_b = lambda i: nl + nc - 1 - i

    def tok(row, col):
        return pl.BlockSpec((1, C, width), lambda b, i: (b, row(i), col))

    const = lambda a: pl.BlockSpec(a.shape, lambda b, i: (0,) * a.ndim)
    out = jax.ShapeDtypeStruct((B, T, width), F32)
    return pl.pallas_call(
        _hgrn_kernel,
        grid=(B, T // C),
        in_specs=[const(mk), const(esel),
                  tok(row_f, 0), tok(row_f, 0), tok(row_f, 0), tok(row_f, 0),
                  tok(row_b, 0), tok(row_b, 1), tok(row_b, 1), tok(row_b, 0)],
        out_specs=[tok(row_f, 0), tok(row_b, 0)],
        out_shape=[out, out],
        scratch_shapes=[pltpu.VMEM((2, HGRN_HEADS, HGRN_DK, HGRN_DK), F32)],
        compiler_params=_cparams(("parallel", "arbitrary")),
        name="hgrn",
    )(jnp.asarray(mk, F32), jnp.asarray(esel, BF16), g3, logf, kk, g2, g3, logf, kk, g2)


def _merge_kernel(x_ref, ya_ref, of_ref, ob_ref, sg_ref, yc_ref, gt_ref, mod_ref, ng_ref, hn_ref,
                  wb_ref, wo_ref, xo_ref, h_ref):
    o = of_ref[0] + ob_ref[0]
    sg = sg_ref[0].astype(F32)
    yb = jnp.concatenate(
        [_rms(o[:, h * LANES:(h + 1) * LANES]) * hn_ref[...] * sg[:, h * LANES:(h + 1) * LANES]
         for h in range(HGRN_HEADS)], axis=-1).astype(BF16)
    ys = (ya_ref[0], yb, yc_ref[0])
    merged = None
    for j in range(3):
        gate = gt_ref[0, :, j * D_MODEL:(j + 1) * D_MODEL].astype(F32)
        term = gate * _dot(ys[j], wb_ref[j])
        merged = term if merged is None else merged + term
    y = _dot(merged.astype(BF16), wo_ref[...])
    x_new = x_ref[0] + _mod_slices(mod_ref, 2) * (_rms(y) * ng_ref[1:2])
    xo_ref[0] = x_new
    h2 = _rms(x_new) * ng_ref[2:3]
    h_ref[0] = (h2 * (1.0 + _mod_slices(mod_ref, 4)) + _mod_slices(mod_ref, 3)).astype(h_ref.dtype)


def _merge(xc, ya, o_f, o_b, g3, yc, g5, modsel, norm_g, hgrn_norm, wb, wo, rows, n_lat_tiles):
    B, _, D = xc.shape
    tm = TOKEN_TILE
    tokw = lambda w, col=0: pl.BlockSpec((1, tm, w), lambda b, i: (b, i, col))
    full = lambda a: pl.BlockSpec(a.shape, lambda b, i: (0,) * a.ndim, pipeline_mode=pl.Buffered(1))
    hn = hgrn_norm.reshape(1, LANES)
    return pl.pallas_call(
        _merge_kernel,
        grid=(B, pl.cdiv(rows, tm)),
        in_specs=[tokw(D), tokw(512), tokw(512), tokw(512), tokw(512, 1), tokw(512), tokw(3 * D),
                  _mod_spec(n_lat_tiles), full(norm_g), full(hn), full(wb), full(wo)],
        out_specs=[tokw(D), tokw(D)],
        out_shape=[jax.ShapeDtypeStruct((B, rows, D), F32), jax.ShapeDtypeStruct((B, rows, D), BF16)],
        compiler_params=_cparams(("parallel", "parallel")),
        name="merge",
    )(xc, ya, o_f, o_b, g3, yc, g5, modsel, norm_g, hn, wb, wo)


def _mlp_kernel(x_ref, h_ref, mod_ref, ng_ref, wu_ref, wd_ref, o_ref):
    u = jnp.maximum(_dot(h_ref[0], wu_ref[...]), 0.0)
    y = _dot((u * u).astype(BF16), wd_ref[...])
    o_ref[0] = x_ref[0] + _mod_slices(mod_ref, 5) * (_rms(y) * ng_ref[3:4])


def _mlp(x_new, h2, modsel, norm_g, wu, wd, n_lat_tiles):
    B, rows, D = x_new.shape
    tm = TOKEN_TILE
    tok = pl.BlockSpec((1, tm, D), lambda b, i: (b, i, 0))
    full = lambda a: pl.BlockSpec(a.shape, lambda b, i: (0,) * a.ndim, pipeline_mode=pl.Buffered(1))
    return pl.pallas_call(
        _mlp_kernel,
        grid=(B, pl.cdiv(rows, tm)),
        in_specs=[tok, tok, _mod_spec(n_lat_tiles), full(norm_g), full(wu), full(wd)],
        out_specs=tok,
        out_shape=jax.ShapeDtypeStruct((B, rows, D), F32),
        compiler_params=_cparams(("parallel", "parallel")),
        name="mlp",
    )(x_new, h2, modsel, norm_g, wu, wd)


def _rope_tables(n_lat):
    t = jnp.arange(n_lat)
    pos = jnp.stack([t // GRID_W, t % GRID_W], axis=-1).astype(F32)
    inv_freq = ROPE_BASE ** (-jnp.arange(ROPE_FREQS, dtype=F32) / ROPE_FREQS)
    ang = pos[:, :, None] * inv_freq
    cos, sin = jnp.cos(ang), jnp.sin(ang)
    cos64 = jnp.concatenate([cos, cos], axis=-1).reshape(n_lat, 4 * ROPE_FREQS)
    sin64 = jnp.concatenate([-sin, sin], axis=-1).reshape(n_lat, 4 * ROPE_FREQS)
    cos_t = jnp.concatenate([jnp.tile(cos64, (1, 2)), jnp.ones((CTX_LEN, LANES), F32)], axis=0)
    sin_t = jnp.concatenate([jnp.tile(sin64, (1, 2)), jnp.zeros((CTX_LEN, LANES), F32)], axis=0)
    return cos_t, sin_t


def _cols(w, *names):
    return [w[:, _OFF[n][0]:_OFF[n][1]] for n in names]


def _dup_heads(w):
    a, b = w[:, :SWA_DIM], w[:, SWA_DIM:]
    return jnp.concatenate([a, a, b, b], axis=1)


def kernel(x, c, ctx, c_ctx, w_ada, b_ada, norm_g, w_in, diff_lambda, diff_subln, hgrn_lb, hgrn_norm,
           swa_sink, w_branch, w_out, w_mlp_up, w_mlp_down):
    B, S, D = x.shape
    depth = w_ada.shape[0]
    assert D == D_MODEL and ctx.shape[1] == CTX_LEN and S % ATT_TQ == 0
    n_lat_tiles = S // TOKEN_TILE

    cos_t, sin_t = _rope_tables(S)
    qscale = jnp.concatenate([jnp.full((1, 512), DIFF_DIM ** -0.5 * LOG2E, F32), jnp.ones((1, 512), F32),
                              jnp.full((1, 512), SWA_DIM ** -0.5 * LOG2E, F32), jnp.ones((1, 256), F32)], axis=1)
    cvec = jnp.concatenate([c, c_ctx[None], jnp.zeros((8 - B - 1, D), F32)], axis=0)
    mods = _ada(cvec, w_ada, b_ada)
    lower = _lower_bounds(hgrn_lb)
    xc = jnp.concatenate([x, ctx], axis=1)

    for l in range(depth):
        need_ctx = l < depth - 1
        lam_init = 0.8 - 0.6 * math.exp(-0.3 * l)
        modsel = jnp.stack([mods[l, :B], jnp.broadcast_to(mods[l, B], (B, 6 * D))], axis=1)[:, :, None, :]
        wl = w_in[l]
        dq, dk, dv, hq, hff, hfb, hi, hg, sq, sk, sv, gt = _cols(
            wl, "dq", "dk", "dv", "hq", "hff", "hfb", "hi", "hg", "sq", "sk", "sv", "gt")
        w1 = jnp.concatenate([dq, dk, sq, _dup_heads(sk)], axis=1).astype(BF16)
        w2 = jnp.concatenate([hi, dv, _dup_heads(sv)], axis=1).astype(BF16)
        w3 = jnp.concatenate([hq, hg], axis=1).astype(BF16)
        w4 = jnp.concatenate([hff, hfb], axis=1).astype(BF16)
        w5 = gt.astype(BF16)

        h = _norm_mod(xc, norm_g[l, 0], modsel, n_lat_tiles)
        g1 = _proj(h, w1, "rope", 896, extra=(cos_t, sin_t, qscale))
        g2 = _proj(h, w2, "plain", 1280)
        g3 = _proj(h, w3, "silu", 1024)
        logf, kk = _proj(h, w4, "hgate", 1024, extra=(lower[l].reshape(1, 1024),))
        g5 = _proj(h, w5, "sigmoid", 1024)

        ya = _diff_attention(g1, g2, diff_lambda[l], diff_subln[l], lam_init, S, latent=True)
        o_f, o_b = _hgrn(g3, logf, kk, g2, S)
        yc = _swa(g1, g2, swa_sink[l], S, latent=True)
        if need_ctx:
            ya = jnp.concatenate(
                [ya, _diff_attention(g1, g2, diff_lambda[l], diff_subln[l], lam_init, S, latent=False)], axis=1)
            yc = jnp.concatenate([yc, _swa(g1, g2, swa_sink[l], S, latent=False)], axis=1)
        rows = S + CTX_LEN if need_ctx else S
        x_new, h2 = _merge(xc, ya, o_f, o_b, g3, yc, g5, modsel, norm_g[l], hgrn_norm[l],
                           w_branch[l].astype(BF16), w_out[l].astype(BF16), rows, n_lat_tiles)
        xc = _mlp(x_new, h2, modsel, norm_g[l], w_mlp_up[l].astype(BF16), w_mlp_down[l].astype(BF16),
                  n_lat_tiles)
    return xc[:, :S]
```
